```python
import math
import jax, jax.numpy as jnp
from jax import lax
import numpy as np

D_MODEL = 1024
BATCH = 8
SEQ = 4096
DEPTH = 1

CHUNK = 64
Q_BLOCK = 128
MIX_WIDTH = D_MODEL
CONV_CH = MIX_WIDTH // 2
CONV_K = 31
N_HEADS = 8
V_HEAD_DIM = (MIX_WIDTH - CONV_CH) // N_HEADS
QK_NOPE_DIM = 64
QK_ROPE_DIM = 32
QK_HEAD_DIM = QK_NOPE_DIM + QK_ROPE_DIM
Q_LORA_RANK = D_MODEL // 4
KV_LORA_RANK = D_MODEL // 8
ROPE_THETA = 10000.0
IN_WIDTH = 2 * CONV_CH + Q_LORA_RANK + KV_LORA_RANK + QK_ROPE_DIM
N_GROUPS = 4
EXPERTS_PER_GROUP = 8
N_EXPERTS = N_GROUPS * EXPERTS_PER_GROUP
TOP_K = 2
D_EXPERT = D_MODEL // 4
EPS = 1e-6

kernel_name = "hybrid_conv_mla_hmoe_block"


def rms_norm(x, g):
    xf = x.astype(jnp.float32)
    y = xf * lax.rsqrt(jnp.mean(xf * xf, axis=-1, keepdims=True) + EPS)
    return (y * g.astype(jnp.float32)).astype(x.dtype)


def layer_norm(x, g, b):
    xf = x.astype(jnp.float32)
    mu = jnp.mean(xf, axis=-1, keepdims=True)
    var = jnp.mean(jnp.square(xf - mu), axis=-1, keepdims=True)
    y = (xf - mu) * lax.rsqrt(var + EPS)
    return (y * g.astype(jnp.float32) + b.astype(jnp.float32)).astype(x.dtype)


def apply_rope(x, cos, sin):
    x1, x2 = jnp.split(x, 2, axis=-1)
    cos = cos.astype(x.dtype)
    sin = sin.astype(x.dtype)
    return jnp.concatenate([x1 * cos - x2 * sin, x1 * sin + x2 * cos], axis=-1)


def conv_group(val, gate, conv_w, conv_b, ln_g, ln_b):
    u = val * jax.nn.sigmoid(gate)
    kern = conv_w[:, None, :].astype(u.dtype)
    dw = lax.conv_general_dilated(
        u, kern, window_strides=(1,), padding=[(CONV_K - 1, 0)],
        dimension_numbers=("NWC", "WIO", "NWC"),
        feature_group_count=CONV_CH) + conv_b.astype(u.dtype)
    return jax.nn.silu(layer_norm(dw, ln_g, ln_b))


def mla_group(c_q, c_kv, k_rope, cos, sin, q_a_norm, w_q_b, kv_a_norm, w_kv_b,
              q_norm, k_norm):
    b, s, _ = c_q.shape
    q = (rms_norm(c_q, q_a_norm) @ w_q_b).reshape(b, s, N_HEADS, QK_HEAD_DIM)
    kv = (rms_norm(c_kv, kv_a_norm) @ w_kv_b).reshape(b, s, N_HEADS, QK_NOPE_DIM + V_HEAD_DIM)
    k_nope, v = kv[..., :QK_NOPE_DIM], kv[..., QK_NOPE_DIM:]
    k_r = jnp.broadcast_to(k_rope[:, :, None, :], (b, s, N_HEADS, QK_ROPE_DIM))
    k = jnp.concatenate([k_nope, k_r], axis=-1)
    q = rms_norm(q, q_norm)
    k = rms_norm(k, k_norm)
    cos_h, sin_h = cos[:, :, None, :], sin[:, :, None, :]
    q = jnp.concatenate([q[..., :QK_NOPE_DIM], apply_rope(q[..., QK_NOPE_DIM:], cos_h, sin_h)], axis=-1)
    k = jnp.concatenate([k[..., :QK_NOPE_DIM], apply_rope(k[..., QK_NOPE_DIM:], cos_h, sin_h)], axis=-1)

    n_blk = s // Q_BLOCK
    key_chunk = jnp.arange(s) // CHUNK
    scale = QK_HEAD_DIM ** -0.5
    q_blocks = q.reshape(b, n_blk, Q_BLOCK, N_HEADS, QK_HEAD_DIM).swapaxes(0, 1)

    def block_attn(args):
        qb, blk = args
        sc = jnp.einsum("bqhd,bkhd->bhqk", qb, k).astype(jnp.float32) * scale
        q_chunk = (blk * Q_BLOCK + jnp.arange(Q_BLOCK)) // CHUNK
        mask = key_chunk[None, :] <= q_chunk[:, None]
        sc = jnp.where(mask[None, None], sc, -jnp.inf)
        p = jax.nn.softmax(sc, axis=-1).astype(v.dtype)
        return jnp.einsum("bhqk,bkhd->bqhd", p, v)

    out = lax.map(block_attn, (q_blocks, jnp.arange(n_blk)))
    return out.swapaxes(0, 1).reshape(b, s, N_HEADS * V_HEAD_DIM)


def hier_moe(h, w_group, b_group, w_expert, b_expert, w_gate_e, w_up_e, w_down_e):
    n = h.shape[0]
    g_logits = (h @ w_group).astype(jnp.float32) + b_group.astype(jnp.float32)
    g_prob = jax.nn.softmax(g_logits, axis=-1)
    g_top, g_idx = lax.top_k(g_prob, 1)
    e_logits = ((h @ w_expert).astype(jnp.float32) + b_expert.astype(jnp.float32)
                ).reshape(n, N_GROUPS, EXPERTS_PER_GROUP)
    sel = jnp.broadcast_to(g_idx[:, :, None], (n, 1, EXPERTS_PER_GROUP))
    e_in = jnp.take_along_axis(e_logits, sel, axis=1)[:, 0]
    e_prob = jax.nn.softmax(e_in, axis=-1)
    e_top, e_idx = lax.top_k(e_prob, TOP_K)
    w = g_top * e_top / jnp.sum(e_top, axis=-1, keepdims=True)
    glob_idx = g_idx * EXPERTS_PER_GROUP + e_idx
    gates = jnp.sum(jax.nn.one_hot(glob_idx, N_EXPERTS, dtype=jnp.float32) * w[..., None],
                    axis=1).astype(h.dtype)
    y = jnp.zeros_like(h)
    for e in range(N_EXPERTS):
        a = jax.nn.silu(h @ w_gate_e[e]) * (h @ w_up_e[e])
        y = y + gates[:, e:e + 1] * (a @ w_down_e[e])
    return y


def hybrid_layer(x, c, cos, sin, w_ada, b_ada, norm_mix, w_in, conv_w, conv_b,
                 conv_ln_g, conv_ln_b, q_a_norm, w_q_b, kv_a_norm, w_kv_b, q_norm,
                 k_norm, w_out, norm_ffn, w_group, b_group, w_expert, b_expert,
                 w_gate_e, w_up_e, w_down_e):
    b, s, d = x.shape
    mod = jax.nn.silu(c) @ w_ada + b_ada
    sh_a, sc_a, g_a, sh_f, sc_f, g_f = jnp.split(mod, 6, axis=-1)

    h = rms_norm(x, norm_mix) * (1 + sc_a[:, None]) + sh_a[:, None]
    proj = h @ w_in
    o1 = CONV_CH
    o2 = o1 + CONV_CH
    o3 = o2 + Q_LORA_RANK
    o4 = o3 + KV_LORA_RANK
    conv_val, conv_gate, c_q, c_kv, k_rope = jnp.split(proj, [o1, o2, o3, o4], axis=-1)
    y_conv = conv_group(conv_val, conv_gate, conv_w, conv_b, conv_ln_g, conv_ln_b)
    y_attn = mla_group(c_q, c_kv, k_rope, cos, sin, q_a_norm, w_q_b, kv_a_norm,
                       w_kv_b, q_norm, k_norm)
    mixed = jnp.concatenate([y_conv, y_attn], axis=-1) @ w_out
    x = x + g_a[:, None] * mixed

    h2 = rms_norm(x, norm_ffn) * (1 + sc_f[:, None]) + sh_f[:, None]
    y = hier_moe(h2.reshape(b * s, d), w_group, b_group, w_expert, b_expert,
                 w_gate_e, w_up_e, w_down_e).reshape(b, s, d)
    return x + g_f[:, None] * y


def setup_inputs(seed: int = 0) -> dict:
    key = jax.random.key(seed)
    ks = jax.random.split(key, 32)
    f32 = jnp.float32

    def nrm(k, shape, scale):
        return jax.random.normal(k, shape, f32) * scale

    def gain(k, n):
        return 1.0 + 0.01 * jax.random.normal(k, (DEPTH, n), f32)

    positions = (jax.random.randint(ks[2], (BATCH, 1), 0, 4096, dtype=jnp.int32)
                 + jnp.arange(SEQ, dtype=jnp.int32)[None, :])
    return {
        "x": nrm(ks[0], (BATCH, SEQ, D_MODEL), 1.0),
        "c": nrm(ks[1], (BATCH, D_MODEL), 1.0),
        "positions": positions,
        "w_ada": nrm(ks[3], (DEPTH, D_MODEL, 6 * D_MODEL), 0.5 * D_MODEL ** -0.5),
        "b_ada": nrm(ks[4], (DEPTH, 6 * D_MODEL), 0.01),
        "norm_mix": gain(ks[5], D_MODEL),
        "w_in": nrm(ks[6], (DEPTH, D_MODEL, IN_WIDTH), D_MODEL ** -0.5),
        "conv_w": nrm(ks[7], (DEPTH, CONV_K, CONV_CH), CONV_K ** -0.5),
        "conv_b": nrm(ks[8], (DEPTH, CONV_CH), 0.01),
        "conv_ln_g": gain(ks[9], CONV_CH),
        "conv_ln_b": nrm(ks[10], (DEPTH, CONV_CH), 0.01),
        "q_a_norm": gain(ks[11], Q_LORA_RANK),
        "w_q_b": nrm(ks[12], (DEPTH, Q_LORA_RANK, N_HEADS * QK_HEAD_DIM), Q_LORA_RANK ** -0.5),
        "kv_a_norm": gain(ks[13], KV_LORA_RANK),
        "w_kv_b": nrm(ks[14], (DEPTH, KV_LORA_RANK, N_HEADS * (QK_NOPE_DIM + V_HEAD_DIM)), KV_LORA_RANK ** -0.5),
        "q_norm": gain(ks[15], QK_HEAD_DIM),
        "k_norm": gain(ks[16], QK_HEAD_DIM),
        "w_out": nrm(ks[17], (DEPTH, MIX_WIDTH, D_MODEL), MIX_WIDTH ** -0.5),
        "norm_ffn": gain(ks[18], D_MODEL),
        "w_group": nrm(ks[19], (DEPTH, D_MODEL, N_GROUPS), D_MODEL ** -0.5),
        "b_group": nrm(ks[20], (DEPTH, N_GROUPS), 0.01),
        "w_expert": nrm(ks[21], (DEPTH, D_MODEL, N_EXPERTS), D_MODEL ** -0.5),
        "b_expert": nrm(ks[22], (DEPTH, N_EXPERTS), 0.01),
        "w_gate_e": nrm(ks[23], (DEPTH, N_EXPERTS, D_MODEL, D_EXPERT), D_MODEL ** -0.5),
        "w_up_e": nrm(ks[24], (DEPTH, N_EXPERTS, D_MODEL, D_EXPERT), D_MODEL ** -0.5),
        "w_down_e": nrm(ks[25], (DEPTH, N_EXPERTS, D_EXPERT, D_MODEL), D_EXPERT ** -0.5),
    }


def reference(x, c, positions, w_ada, b_ada, norm_mix, w_in, conv_w, conv_b,
              conv_ln_g, conv_ln_b, q_a_norm, w_q_b, kv_a_norm, w_kv_b, q_norm,
              k_norm, w_out, norm_ffn, w_group, b_group, w_expert, b_expert,
              w_gate_e, w_up_e, w_down_e):
    inv_freq = ROPE_THETA ** (-jnp.arange(0, QK_ROPE_DIM, 2, dtype=jnp.float32) / QK_ROPE_DIM)
    ang = positions.astype(jnp.float32)[..., None] * inv_freq
    cos, sin = jnp.cos(ang), jnp.sin(ang)
    for l in range(DEPTH):
        x = hybrid_layer(x, c, cos, sin, w_ada[l], b_ada[l], norm_mix[l], w_in[l],
                         conv_w[l], conv_b[l], conv_ln_g[l], conv_ln_b[l],
                         q_a_norm[l], w_q_b[l], kv_a_norm[l], w_kv_b[l], q_norm[l],
                         k_norm[l], w_out[l], norm_ffn[l], w_group[l], b_group[l],
                         w_expert[l], b_expert[l], w_gate_e[l], w_up_e[l], w_down_e[l])
    return x
```

```python
import functools

import jax
import jax.numpy as jnp
from jax import lax
from jax.experimental import pallas as pl
from jax.experimental.pallas import tpu as pltpu

F32 = jnp.float32
BF16 = jnp.bfloat16

CHUNK = 64
CONV_K = 31
N_HEADS = 8
V_HEAD_DIM = 64
QK_NOPE_DIM = 64
QK_ROPE_DIM = 32
QK_HEAD_DIM = QK_NOPE_DIM + QK_ROPE_DIM
N_GROUPS = 4
EXPERTS_PER_GROUP = 8
N_EXPERTS = N_GROUPS * EXPERTS_PER_GROUP
TOP_K = 2
ROPE_THETA = 10000.0
EPS = 1e-6

LANES = 128
CONV_HALO = 32

TM_PROJ = 512
TS_CONV = 256
CONV_ROWS = 32
TQ_ATTN = 256
T_MOE = 256
TM_COMB = 256
VMEM_LIMIT = 48 * 1024 * 1024


def _sigmoid(v):
    return 1.0 / (1.0 + jnp.exp(-v))


def _cparams(sem):
    return pltpu.CompilerParams(dimension_semantics=sem, vmem_limit_bytes=VMEM_LIMIT)


def _ada_kernel(c_ref, w_ref, b_ref, o_ref):
    c = c_ref[...]
    s = c * _sigmoid(c)
    o_ref[...] = jnp.dot(s, w_ref[...], preferred_element_type=F32,
                         precision=lax.Precision.HIGHEST) + b_ref[...]


def _ada(c, w_ada, b_ada):
    b, d = c.shape
    n_out = w_ada.shape[1]
    return pl.pallas_call(
        _ada_kernel,
        grid=(n_out // d,),
        in_specs=[pl.BlockSpec((b, d), lambda j: (0, 0)),
                  pl.BlockSpec((d, d), lambda j: (0, j)),
                  pl.BlockSpec((1, d), lambda j: (0, j))],
        out_specs=pl.BlockSpec((b, d), lambda j: (0, j)),
        out_shape=jax.ShapeDtypeStruct((b, n_out), F32),
        compiler_params=_cparams(("arbitrary",)),
        name="ada",
    )(c, w_ada, b_ada.reshape(1, n_out))


def _head_norm_rope(t, gain, cosf, sinf, lane):
    r = lax.rsqrt(jnp.sum(t * t, axis=-1, keepdims=True) * (1.0 / QK_HEAD_DIM) + EPS)
    tn = t * r * gain
    half = QK_ROPE_DIM // 2
    swapped = jnp.where(lane < QK_NOPE_DIM + half,
                        pltpu.roll(tn, LANES - half, 1), pltpu.roll(tn, half, 1))
    return tn * cosf + swapped * sinf


def _inproj_kernel(x_ref, mod_ref, nmix_ref, win_ref, qan_ref, wq_ref, kvan_ref, wk_ref, wv_ref,
                   qn_ref, kn_ref, cos_ref, sin_ref, u_ref, q_ref, k_ref, v_ref):
    x = x_ref[...]
    sh_a = mod_ref[0, 0:1, :]
    sc_a = mod_ref[0, 1:2, :]
    y = x * lax.rsqrt(jnp.mean(x * x, axis=-1, keepdims=True) + EPS) * nmix_ref[...]
    h = y * (1.0 + sc_a) + sh_a
    proj = jnp.dot(h.astype(BF16), win_ref[...], preferred_element_type=F32)
    c_conv = u_ref.shape[-1]
    val = proj[:, :c_conv]
    gate = proj[:, c_conv:2 * c_conv]
    u_ref[...] = (val * _sigmoid(gate)).astype(BF16)
    o = 2 * c_conv
    q_rank = wq_ref.shape[0]
    kv_rank = wk_ref.shape[0]
    cq = proj[:, o:o + q_rank]
    ckv = proj[:, o + q_rank:o + q_rank + kv_rank]
    kr = proj[:, o + q_rank + kv_rank:o + q_rank + kv_rank + LANES]
    cqn = cq * lax.rsqrt(jnp.mean(cq * cq, axis=-1, keepdims=True) + EPS) * qan_ref[...]
    ckvn = ckv * lax.rsqrt(jnp.mean(ckv * ckv, axis=-1, keepdims=True) + EPS) * kvan_ref[...]
    q_all = jnp.dot(cqn.astype(BF16), wq_ref[...], preferred_element_type=F32)
    ckvn_b = ckvn.astype(BF16)
    k_all = jnp.dot(ckvn_b, wk_ref[...], preferred_element_type=F32)
    v_ref[...] = jnp.dot(ckvn_b, wv_ref[...], preferred_element_type=F32).astype(BF16)
    cosf = cos_ref[...]
    sinf = sin_ref[...]
    lane = lax.broadcasted_iota(jnp.int32, cosf.shape, 1)
    scale = QK_HEAD_DIM ** -0.5
    for hd in range(N_HEADS):
        qh = q_all[:, hd * LANES:(hd + 1) * LANES]
        q_ref[0, hd] = (_head_norm_rope(qh, qn_ref[...], cosf, sinf, lane) * scale).astype(BF16)
        kh = k_all[:, hd * LANES:(hd + 1) * LANES] + kr
        k_ref[0, hd] = _head_norm_rope(kh, kn_ref[...], cosf, sinf, lane).astype(BF16)


def _inproj(x2, mod3, nmix, win_p, qan, wq_p, kvan, wk_p, wv, qn_p, kn_p, cosf, sinf, b, s):
    n, d = x2.shape
    tm = TM_PROJ
    tpb = s // tm
    c_conv = wv.shape[1]
    full = lambda a: pl.BlockSpec(a.shape, lambda i: (0,) * a.ndim)
    return pl.pallas_call(
        _inproj_kernel,
        grid=(n // tm,),
        in_specs=[pl.BlockSpec((tm, d), lambda i: (i, 0)),
                  pl.BlockSpec((1, 6, d), lambda i: (i // tpb, 0, 0)),
                  full(nmix), full(win_p), full(qan), full(wq_p), full(kvan), full(wk_p), full(wv),
                  full(qn_p), full(kn_p),
                  pl.BlockSpec((tm, LANES), lambda i: (i, 0)),
                  pl.BlockSpec((tm, LANES), lambda i: (i, 0))],
        out_specs=[pl.BlockSpec((tm, c_conv), lambda i: (i, 0)),
                   pl.BlockSpec((1, N_HEADS, tm, LANES), lambda i: (i // tpb, 0, i % tpb, 0)),
                   pl.BlockSpec((1, N_HEADS, tm, LANES), lambda i: (i // tpb, 0, i % tpb, 0)),
                   pl.BlockSpec((tm, c_conv), lambda i: (i, 0))],
        out_shape=[jax.ShapeDtypeStruct((n, c_conv), BF16),
                   jax.ShapeDtypeStruct((b, N_HEADS, s, LANES), BF16),
                   jax.ShapeDtypeStruct((b, N_HEADS, s, LANES), BF16),
                   jax.ShapeDtypeStruct((n, c_conv), BF16)],
        compiler_params=_cparams(("arbitrary",)),
        name="inproj",
    )(x2, mod3, nmix, win_p, qan, wq_p, kvan, wk_p, wv, qn_p, kn_p, cosf, sinf)


def _conv_kernel(prev_ref, cur_ref, w_ref, b_ref, g_ref, beta_ref, o_ref, buf_ref):
    j = pl.program_id(1)
    ts = cur_ref.shape[1]

    @pl.when(j == 0)
    def _():
        buf_ref[0:CONV_HALO, :] = jnp.zeros((CONV_HALO, buf_ref.shape[1]), F32)

    @pl.when(j > 0)
    def _():
        buf_ref[0:CONV_HALO, :] = prev_ref[0].astype(F32)

    buf_ref[CONV_HALO:CONV_HALO + ts, :] = cur_ref[0].astype(F32)
    first = CONV_HALO - (CONV_K - 1)
    for r0 in range(0, ts, CONV_ROWS):
        acc = buf_ref[r0 + first:r0 + first + CONV_ROWS, :] * w_ref[0:1, :]
        for kk in range(1, CONV_K):
            acc = acc + buf_ref[r0 + first + kk:r0 + first + kk + CONV_ROWS, :] * w_ref[kk:kk + 1, :]
        dw = acc + b_ref[...]
        mu = jnp.mean(dw, axis=-1, keepdims=True)
        cen = dw - mu
        var = jnp.mean(cen * cen, axis=-1, keepdims=True)
        z = cen * lax.rsqrt(var + EPS) * g_ref[...] + beta_ref[...]
        o_ref[0, r0:r0 + CONV_ROWS, :] = (z * _sigmoid(z)).astype(BF16)


def _conv(u3, conv_w_p, conv_b, ln_g, ln_b):
    b, s, c = u3.shape
    ts = TS_CONV
    hb = ts // CONV_HALO
    return pl.pallas_call(
        _conv_kernel,
        grid=(b, s // ts),
        in_specs=[pl.BlockSpec((1, CONV_HALO, c), lambda bi, j: (bi, jnp.maximum(j * hb - 1, 0), 0)),
                  pl.BlockSpec((1, ts, c), lambda bi, j: (bi, j, 0)),
                  pl.BlockSpec(conv_w_p.shape, lambda bi, j: (0, 0)),
                  pl.BlockSpec((1, c), lambda bi, j: (0, 0)),
                  pl.BlockSpec((1, c), lambda bi, j: (0, 0)),
                  pl.BlockSpec((1, c), lambda bi, j: (0, 0))],
        out_specs=pl.BlockSpec((1, ts, c), lambda bi, j: (bi, j, 0)),
        out_shape=jax.ShapeDtypeStruct((b, s, c), BF16),
        scratch_shapes=[pltpu.VMEM((CONV_HALO + ts, c), F32)],
        compiler_params=_cparams(("arbitrary", "arbitrary")),
        name="conv",
    )(u3, u3, conv_w_p, conv_b, ln_g, ln_b)


def _attn_kernel(q_ref, k_ref, v_ref, o_ref):
    qi = pl.program_id(2)
    tq = q_ref.shape[2]
    row_chunk = lax.broadcasted_iota(jnp.int32, (tq, tq), 0) // CHUNK
    col_chunk = lax.broadcasted_iota(jnp.int32, (tq, tq), 1) // CHUNK
    diag_mask = col_chunk <= row_chunk
    outs = []
    for hh in range(2):
        q = q_ref[0, hh]

        def tile(start, carry, masked):
            m, l, acc = carry
            kj = k_ref[0, hh, pl.ds(start, tq), :]
            vj = v_ref[0, pl.ds(start, tq), :]
            sc = lax.dot_general(q, kj, (((1,), (1,)), ((), ())), preferred_element_type=F32)
            if masked:
                sc = jnp.where(diag_mask, sc, -jnp.inf)
            m_new = jnp.maximum(m, jnp.max(sc, axis=-1, keepdims=True))
            alpha = jnp.exp(m - m_new)
            p = jnp.exp(sc - m_new)
            l_new = alpha * l + jnp.sum(p, axis=-1, keepdims=True)
            acc_new = alpha * acc + jnp.dot(p.astype(BF16), vj, preferred_element_type=F32)
            return m_new, l_new, acc_new

        def body(jt, carry):
            return tile(pl.multiple_of(jt * tq, tq), carry, False)

        init = (jnp.full((tq, 1), -jnp.inf, F32), jnp.zeros((tq, 1), F32), jnp.zeros((tq, LANES), F32))
        carry = lax.fori_loop(0, qi, body, init)
        m, l, acc = tile(pl.multiple_of(qi * tq, tq), carry, True)
        outs.append(acc / l)
    lane = lax.broadcasted_iota(jnp.int32, (tq, LANES), 1)
    o_ref[0] = jnp.where(lane < V_HEAD_DIM, outs[0], outs[1]).astype(BF16)


def _attn(q4, k4, v3):
    b, nh, s, _ = q4.shape
    tq = TQ_ATTN
    return pl.pallas_call(
        _attn_kernel,
        grid=(b, nh // 2, s // tq),
        in_specs=[pl.BlockSpec((1, 2, tq, LANES), lambda bi, hp, qi: (bi, hp, qi, 0)),
                  pl.BlockSpec((1, 2, s, LANES), lambda bi, hp, qi: (bi, hp, 0, 0)),
                  pl.BlockSpec((1, s, LANES), lambda bi, hp, qi: (bi, 0, hp))],
        out_specs=pl.BlockSpec((1, tq, LANES), lambda bi, hp, qi: (bi, qi, hp)),
        out_shape=jax.ShapeDtypeStruct((b, s, nh * V_HEAD_DIM), BF16),
        compiler_params=_cparams(("arbitrary", "arbitrary", "arbitrary")),
        name="attn",
    )(q4, k4, v3)


def _outproj_kernel(yc_ref, ya_ref, x_ref, mod_ref, wout_ref, nffn_ref, wr_ref, br_ref,
                    x1_ref, h2_ref, route_ref):
    g_a = mod_ref[0, 2:3, :]
    sh_f = mod_ref[0, 3:4, :]
    sc_f = mod_ref[0, 4:5, :]
    ycat = jnp.concatenate([yc_ref[...], ya_ref[...]], axis=-1)
    mixed = jnp.dot(ycat, wout_ref[...], preferred_element_type=F32)
    x1 = x_ref[...] + g_a * mixed
    x1_ref[...] = x1
    y = x1 * lax.rsqrt(jnp.mean(x1 * x1, axis=-1, keepdims=True) + EPS) * nffn_ref[...]
    h2 = y * (1.0 + sc_f) + sh_f
    h2_ref[...] = h2
    logits = jnp.dot(h2.astype(BF16), wr_ref[...], preferred_element_type=F32) + br_ref[...]
    lane = lax.broadcasted_iota(jnp.int32, logits.shape, 1)
    lane_f = lane.astype(F32)
    big = float(LANES)
    is_g = (lane >= N_EXPERTS) & (lane < N_EXPERTS + N_GROUPS)
    gl = jnp.where(is_g, logits, -jnp.inf)
    gmax = jnp.max(gl, axis=-1, keepdims=True)
    g_top = 1.0 / jnp.sum(jnp.exp(gl - gmax), axis=-1, keepdims=True)
    g_idx = jnp.min(jnp.where(gl == gmax, lane_f - N_EXPERTS, big), axis=-1, keepdims=True)
    in_grp = (lane < N_EXPERTS) & ((lane // EXPERTS_PER_GROUP).astype(F32) == g_idx)
    el = jnp.where(in_grp, logits, -jnp.inf)
    e1 = jnp.max(el, axis=-1, keepdims=True)
    i1 = jnp.min(jnp.where(el == e1, lane_f, big), axis=-1, keepdims=True)
    el2 = jnp.where(lane_f == i1, -jnp.inf, el)
    e2 = jnp.max(el2, axis=-1, keepdims=True)
    i2 = jnp.min(jnp.where(el2 == e2, lane_f, big), axis=-1, keepdims=True)
    t = jnp.exp(e2 - e1)
    w1 = g_top / (1.0 + t)
    w2 = g_top * t / (1.0 + t)
    route_ref[...] = jnp.where(lane == 0, w1, jnp.where(lane == 1, w2, jnp.where(
        lane == 2, i1, jnp.where(lane == 3, i2, 0.0))))


def _outproj(yc, ya, x2, mod3, wout, nffn, wr, br, s):
    n, d = x2.shape
    tm = TM_PROJ
    tpb = s // tm
    c = yc.shape[1]
    full = lambda a: pl.BlockSpec(a.shape, lambda i: (0,) * a.ndim)
    return pl.pallas_call(
        _outproj_kernel,
        grid=(n // tm,),
        in_specs=[pl.BlockSpec((tm, c), lambda i: (i, 0)),
                  pl.BlockSpec((tm, c), lambda i: (i, 0)),
                  pl.BlockSpec((tm, d), lambda i: (i, 0)),
                  pl.BlockSpec((1, 6, d), lambda i: (i // tpb, 0, 0)),
                  full(wout), full(nffn), full(wr), full(br)],
        out_specs=[pl.BlockSpec((tm, d), lambda i: (i, 0)),
                   pl.BlockSpec((tm, d), lambda i: (i, 0)),
                   pl.BlockSpec((tm, LANES), lambda i: (i, 0))],
        out_shape=[jax.ShapeDtypeStruct((n, d), F32),
                   jax.ShapeDtypeStruct((n, d), F32),
                   jax.ShapeDtypeStruct((n, LANES), F32)],
        compiler_params=_cparams(("arbitrary",)),
        name="outproj",
    )(yc, ya, x2, mod3, wout, nffn, wr, br)


def _gather_rows_step(i, n_steps, idx_hbm, src_hbm, ismem, buf, isem, rsem):
    n_rows = ismem.shape[1]
    slot = i % 2
    nslot = 1 - slot

    def idx_copy(step, s):
        return pltpu.make_async_copy(idx_hbm.at[step], ismem.at[s], isem.at[s])

    def row_copy(tok, r, s):
        return pltpu.make_async_copy(src_hbm.at[pl.ds(tok, 1)], buf.at[s, pl.ds(r, 1)], rsem.at[s])

    def issue_rows(s):
        def body(r, carry):
            row_copy(ismem[s, r], r, s).start()
            return carry
        lax.fori_loop(0, n_rows, body, 0, unroll=8)

    def wait_rows(s):
        def body(r, carry):
            row_copy(0, r, s).wait()
            return carry
        lax.fori_loop(0, n_rows, body, 0, unroll=8)

    @pl.when(i == 0)
    def _():
        idx_copy(0, 0).start()
        idx_copy(0, 0).wait()
        issue_rows(0)

        @pl.when(n_steps > 1)
        def _():
            idx_copy(1, 1).start()

    @pl.when(i + 1 < n_steps)
    def _():
        idx_copy(i + 1, nslot).wait()
        issue_rows(nslot)

    @pl.when(i + 2 < n_steps)
    def _():
        idx_copy(i + 2, slot).start()

    wait_rows(slot)


def _moe_kernel(te_ref, nv_ref, rt_hbm, h2_hbm, wg_ref, wu_ref, wd_ref, y_ref,
                ismem, xbuf, isem, rsem):
    i = pl.program_id(0)
    n_valid = nv_ref[0]

    @pl.when(i < n_valid)
    def _():
        _gather_rows_step(i, n_valid, rt_hbm, h2_hbm, ismem, xbuf, isem, rsem)
        xg = xbuf[i % 2].astype(BF16)
        g = jnp.dot(xg, wg_ref[0], preferred_element_type=F32)
        u = jnp.dot(xg, wu_ref[0], preferred_element_type=F32)
        a = (g * _sigmoid(g) * u).astype(BF16)
        y_ref[...] = jnp.dot(a, wd_ref[0], preferred_element_type=F32)

    @pl.when(i >= n_valid)
    def _():
        y_ref[...] = jnp.zeros(y_ref.shape, F32)


def _moe(tile_expert, n_valid, row_token2, h2, wg, wu, wd):
    n_tiles, t = row_token2.shape
    d = h2.shape[1]
    de = wg.shape[2]
    grid_spec = pltpu.PrefetchScalarGridSpec(
        num_scalar_prefetch=2,
        grid=(n_tiles,),
        in_specs=[pl.BlockSpec(memory_space=pl.ANY),
                  pl.BlockSpec(memory_space=pl.ANY),
                  pl.BlockSpec((1, d, de), lambda i, te, nv: (te[i], 0, 0)),
                  pl.BlockSpec((1, d, de), lambda i, te, nv: (te[i], 0, 0)),
                  pl.BlockSpec((1, de, d), lambda i, te, nv: (te[i], 0, 0))],
        out_specs=pl.BlockSpec((t, d), lambda i, te, nv: (i, 0)),
        scratch_shapes=[pltpu.SMEM((2, t), jnp.int32),
                        pltpu.VMEM((2, t, d), F32),
                        pltpu.SemaphoreType.DMA((2,)),
                        pltpu.SemaphoreType.DMA((2,))])
    return pl.pallas_call(
        _moe_kernel,
        grid_spec=grid_spec,
        out_shape=jax.ShapeDtypeStruct((n_tiles * t, d), F32),
        compiler_params=_cparams(("arbitrary",)),
        name="moe",
    )(tile_expert, n_valid, row_token2, h2, wg, wu, wd)


def _combine_kernel(pos_hbm, y_hbm, x1_ref, route_ref, mod_ref, o_ref, ismem, ybuf, isem, rsem):
    i = pl.program_id(0)
    tm = x1_ref.shape[0]
    _gather_rows_step(i, pl.num_programs(0), pos_hbm, y_hbm, ismem, ybuf, isem, rsem)
    slot = i % 2
    g_f = mod_ref[0, 5:6, :]
    w0 = route_ref[:, 0:1]
    w1 = route_ref[:, 1:2]
    y = w0 * ybuf[slot, 0:tm, :] + w1 * ybuf[slot, tm:2 * tm, :]
    o_ref[...] = x1_ref[...] + g_f * y


def _combine(pos2, y_sorted, x1, route, mod3, s):
    n, d = x1.shape
    tm = TM_COMB
    tpb = s // tm
    return pl.pallas_call(
        _combine_kernel,
        grid=(n // tm,),
        in_specs=[pl.BlockSpec(memory_space=pl.ANY),
                  pl.BlockSpec(memory_space=pl.ANY),
                  pl.BlockSpec((tm, d), lambda i: (i, 0)),
                  pl.BlockSpec((tm, LANES), lambda i: (i, 0)),
                  pl.BlockSpec((1, 6, d), lambda i: (i // tpb, 0, 0))],
        out_specs=pl.BlockSpec((tm, d), lambda i: (i, 0)),
        out_shape=jax.ShapeDtypeStruct((n, d), F32),
        scratch_shapes=[pltpu.SMEM((2, 2 * tm), jnp.int32),
                        pltpu.VMEM((2, 2 * tm, d), F32),
                        pltpu.SemaphoreType.DMA((2,)),
                        pltpu.SemaphoreType.DMA((2,))],
        compiler_params=_cparams(("arbitrary",)),
        name="combine",
    )(pos2, y_sorted, x1, route, mod3)


def _routing_tables(route, n):
    t = T_MOE
    flat_e = route[:, 2:4].astype(jnp.int32).reshape(-1)
    onehot = (flat_e[:, None] == jnp.arange(N_EXPERTS, dtype=jnp.int32)[None, :]).astype(jnp.int32)
    csum = jnp.cumsum(onehot, axis=0)
    rank = jnp.sum(onehot * csum, axis=1) - 1
    counts = csum[-1]
    ptiles = (counts + t - 1) // t
    tile_end = jnp.cumsum(ptiles)
    tile_start = tile_end - ptiles
    pos = jnp.sum(onehot * (tile_start * t)[None, :], axis=1) + rank
    n_tiles = (TOP_K * n) // t + N_EXPERTS
    n_valid = tile_end[-1]
    tile_ids = jnp.arange(n_tiles, dtype=jnp.int32)
    te = jnp.sum((tile_ids[:, None] >= tile_end[None, :]).astype(jnp.int32), axis=1)
    te_last = jnp.sum((n_valid - 1 >= tile_end).astype(jnp.int32))
    tile_expert = jnp.minimum(jnp.where(tile_ids < n_valid, te, te_last), N_EXPERTS - 1)
    row_token = jnp.zeros((n_tiles * t,), jnp.int32).at[pos].set(
        jnp.arange(TOP_K * n, dtype=jnp.int32) // TOP_K)
    pos2 = pos.reshape(n // TM_COMB, TM_COMB, TOP_K).transpose(0, 2, 1).reshape(n // TM_COMB, TOP_K * TM_COMB)
    return (tile_expert.astype(jnp.int32), n_valid.astype(jnp.int32).reshape(1),
            row_token.reshape(n_tiles, t), pos2.astype(jnp.int32))


def _pad_lanes(a, left, total):
    pad = [(0, 0)] * (a.ndim - 1) + [(left, total - left - a.shape[-1])]
    return jnp.pad(a, pad)


def _layer(x2, c, cosf, sinf, b, s, w_ada, b_ada, norm_mix, w_in, conv_w, conv_b, conv_ln_g, conv_ln_b,
           q_a_norm, w_q_b, kv_a_norm, w_kv_b, q_norm, k_norm, w_out, norm_ffn, w_group, b_group,
           w_expert, b_expert, w_gate_e, w_up_e, w_down_e):
    n, d = x2.shape
    c_conv = conv_w.shape[1]
    q_rank = q_a_norm.shape[0]
    kv_rank = kv_a_norm.shape[0]
    row = lambda a: a.reshape(1, -1)

    o1 = 2 * c_conv
    o2 = o1 + q_rank
    o3 = o2 + kv_rank
    win_p = jnp.concatenate([w_in[:, :o3], _pad_lanes(w_in[:, o3:], QK_NOPE_DIM, LANES)], axis=1).astype(BF16)
    wq_p = _pad_lanes(w_q_b.reshape(q_rank, N_HEADS, QK_HEAD_DIM), 0, LANES).reshape(q_rank, N_HEADS * LANES)
    wkv = w_kv_b.reshape(kv_rank, N_HEADS, QK_NOPE_DIM + V_HEAD_DIM)
    wk_p = _pad_lanes(wkv[..., :QK_NOPE_DIM], 0, LANES).reshape(kv_rank, N_HEADS * LANES)
    wv = wkv[..., QK_NOPE_DIM:].reshape(kv_rank, N_HEADS * V_HEAD_DIM)
    qn_p = _pad_lanes(row(q_norm), 0, LANES)
    kn_p = _pad_lanes(row(k_norm), 0, LANES)
    wr = _pad_lanes(jnp.concatenate([w_expert, w_group], axis=1), 0, LANES).astype(BF16)
    br = _pad_lanes(row(jnp.concatenate([b_expert, b_group])), 0, LANES)
    conv_w_p = jnp.pad(conv_w, ((0, CONV_HALO - CONV_K), (0, 0)))

    mod3 = _ada(c, w_ada, b_ada).reshape(b, 6, d)
    u, q4, k4, v = _inproj(x2, mod3, row(norm_mix), win_p, row(q_a_norm), wq_p.astype(BF16),
                           row(kv_a_norm), wk_p.astype(BF16), wv.astype(BF16), qn_p, kn_p,
                           cosf, sinf, b, s)
    y_conv = _conv(u.reshape(b, s, c_conv), conv_w_p, row(conv_b), row(conv_ln_g), row(conv_ln_b))
    y_attn = _attn(q4, k4, v.reshape(b, s, -1))
    x1, h2, route = _outproj(y_conv.reshape(n, c_conv), y_attn.reshape(n, -1), x2, mod3,
                             w_out.astype(BF16), row(norm_ffn), wr, br, s)
    tile_expert, n_valid, row_token2, pos2 = _routing_tables(route, n)
    y_sorted = _moe(tile_expert, n_valid, row_token2, h2,
                    w_gate_e.astype(BF16), w_up_e.astype(BF16), w_down_e.astype(BF16))
    return _combine(pos2, y_sorted, x1, route, mod3, s)


def kernel(x, c, positions, w_ada, b_ada, norm_mix, w_in, conv_w, conv_b, conv_ln_g, conv_ln_b, q_a_norm,
           w_q_b, kv_a_norm, w_kv_b, q_norm, k_norm, w_out, norm_ffn, w_group, b_group, w_expert, b_expert,
           w_gate_e, w_up_e, w_down_e):
    b, s, d = x.shape
    n = b * s
    half = QK_ROPE_DIM // 2
    inv_freq = ROPE_THETA ** (-jnp.arange(0, QK_ROPE_DIM, 2, dtype=F32) / QK_ROPE_DIM)
    ang = positions.astype(F32).reshape(n, 1) * inv_freq[None, :]
    cos, sin = jnp.cos(ang), jnp.sin(ang)
    cosf = jnp.concatenate([jnp.ones((n, QK_NOPE_DIM), F32), cos, cos,
                            jnp.zeros((n, LANES - QK_HEAD_DIM), F32)], axis=1)
    sinf = jnp.concatenate([jnp.zeros((n, QK_NOPE_DIM), F32), -sin, sin,
                            jnp.zeros((n, LANES - QK_HEAD_DIM), F32)], axis=1)
    x2 = x.reshape(n, d)
    for l in range(w_ada.shape[0]):
        x2 = _layer(x2, c, cosf, sinf, b, s, w_ada[l], b_ada[l], norm_mix[l], w_in[l], conv_w[l], conv_b[l],
                    conv_ln_g[l], conv_ln_b[l], q_a_norm[l], w_q_b[l], kv_a_norm[l], w_kv_b[l], q_norm[l],
                    k_norm[l], w_out[l], norm_ffn[l], w_group[l], b_group[l], w_expert[l], b_expert[l],
                    w_gate_e[l], w_up_e[l], w_down_e[l])
    return x2.reshape(b, s, d)
```

```python
import functools

import jax
import jax.numpy as jnp
from jax import lax
from jax.experimental import pallas as pl
from jax.experimental.pallas import tpu as pltpu

F32 = jnp.float32
BF16 = jnp.bfloat16

CHUNK = 64
CONV_K = 31
N_HEADS = 8
V_HEAD_DIM = 64
QK_NOPE_DIM = 64
QK_ROPE_DIM = 32
QK_HEAD_DIM = QK_NOPE_DIM + QK_ROPE_DIM
N_GROUPS = 4
EXPERTS_PER_GROUP = 8
N_EXPERTS = N_GROUPS * EXPERTS_PER_GROUP
TOP_K = 2
ROPE_THETA = 10000.0
EPS = 1e-6
LOG2_E = 1.4426950408889634

LANES = 128
CONV_HALO = 32

TM_PROJ = 512
TS_CONV = 256
CONV_ROWS = 32
TQ_ATTN = 256
HEADS_PER_STEP = 4
T_MOE = 256
TM_COMB = 256
VMEM_LIMIT = 48 * 1024 * 1024


def _sigmoid(v):
    return 1.0 / (1.0 + jnp.exp(-v))


def _cparams(sem):
    return pltpu.CompilerParams(dimension_semantics=sem, vmem_limit_bytes=VMEM_LIMIT)


def _ada_kernel(c_ref, w_ref, b_ref, o_ref):
    c = c_ref[...]
    s = c * _sigmoid(c)
    o_ref[...] = jnp.dot(s, w_ref[...], preferred_element_type=F32,
                         precision=lax.Precision.HIGHEST) + b_ref[...]


def _ada(c, w_ada, b_ada):
    b, d = c.shape
    n_out = w_ada.shape[1]
    return pl.pallas_call(
        _ada_kernel,
        grid=(n_out // d,),
        in_specs=[pl.BlockSpec((b, d), lambda j: (0, 0)),
                  pl.BlockSpec((d, d), lambda j: (0, j)),
                  pl.BlockSpec((1, d), lambda j: (0, j))],
        out_specs=pl.BlockSpec((b, d), lambda j: (0, j)),
        out_shape=jax.ShapeDtypeStruct((b, n_out), F32),
        compiler_params=_cparams(("arbitrary",)),
        name="ada",
    )(c, w_ada, b_ada.reshape(1, n_out))


def _head_norm_rope(t, gain, cosf, sinf, lane):
    r = lax.rsqrt(jnp.sum(t * t, axis=-1, keepdims=True) * (1.0 / QK_HEAD_DIM) + EPS)
    tn = t * r * gain
    half = QK_ROPE_DIM // 2
    swapped = jnp.where(lane < QK_NOPE_DIM + half,
                        pltpu.roll(tn, LANES - half, 1), pltpu.roll(tn, half, 1))
    return tn * cosf + swapped * sinf


def _inproj_kernel(x_ref, mod_ref, nmix_ref, win_ref, qan_ref, wq_ref, kvan_ref, wk_ref, wv_ref,
                   qn_ref, kn_ref, cos_ref, sin_ref, u_ref, qt_ref, k_ref, vt_ref):
    x = x_ref[...]
    sh_a = mod_ref[0, 0:1, :]
    sc_a = mod_ref[0, 1:2, :]
    y = x * lax.rsqrt(jnp.mean(x * x, axis=-1, keepdims=True) + EPS) * nmix_ref[...]
    h = y * (1.0 + sc_a) + sh_a
    proj = jnp.dot(h.astype(BF16), win_ref[...], preferred_element_type=F32)
    c_conv = u_ref.shape[-1]
    val = proj[:, :c_conv]
    gate = proj[:, c_conv:2 * c_conv]
    u_ref[...] = (val * _sigmoid(gate)).astype(BF16)
    o = 2 * c_conv
    q_rank = wq_ref.shape[0]
    kv_rank = wk_ref.shape[0]
    cq = proj[:, o:o + q_rank]
    ckv = proj[:, o + q_rank:o + q_rank + kv_rank]
    kr = proj[:, o + q_rank + kv_rank:o + q_rank + kv_rank + LANES]
    cqn = cq * lax.rsqrt(jnp.mean(cq * cq, axis=-1, keepdims=True) + EPS) * qan_ref[...]
    ckvn = ckv * lax.rsqrt(jnp.mean(ckv * ckv, axis=-1, keepdims=True) + EPS) * kvan_ref[...]
    q_all = jnp.dot(cqn.astype(BF16), wq_ref[...], preferred_element_type=F32)
    ckvn_b = ckvn.astype(BF16)
    k_all = jnp.dot(ckvn_b, wk_ref[...], preferred_element_type=F32)
    vt_ref[0] = jnp.dot(ckvn_b, wv_ref[...], preferred_element_type=F32).T.astype(BF16)
    cosf = cos_ref[...]
    sinf = sin_ref[...]
    lane = lax.broadcasted_iota(jnp.int32, cosf.shape, 1)
    scale = QK_HEAD_DIM ** -0.5 * LOG2_E
    for hd in range(N_HEADS):
        qh = q_all[:, hd * LANES:(hd + 1) * LANES]
        qt_ref[0, hd] = (_head_norm_rope(qh, qn_ref[...], cosf, sinf, lane) * scale).T.astype(BF16)
        kh = k_all[:, hd * LANES:(hd + 1) * LANES] + kr
        k_ref[0, hd] = _head_norm_rope(kh, kn_ref[...], cosf, sinf, lane).astype(BF16)


def _inproj(x2, mod3, nmix, win_p, qan, wq_p, kvan, wk_p, wv, qn_p, kn_p, cosf, sinf, b, s):
    n, d = x2.shape
    tm = TM_PROJ
    tpb = s // tm
    c_conv = (win_p.shape[1] - wq_p.shape[0] - wk_p.shape[0] - LANES) // 2
    full = lambda a: pl.BlockSpec(a.shape, lambda i: (0,) * a.ndim)
    return pl.pallas_call(
        _inproj_kernel,
        grid=(n // tm,),
        in_specs=[pl.BlockSpec((tm, d), lambda i: (i, 0)),
                  pl.BlockSpec((1, 6, d), lambda i: (i // tpb, 0, 0)),
                  full(nmix), full(win_p), full(qan), full(wq_p), full(kvan), full(wk_p), full(wv),
                  full(qn_p), full(kn_p),
                  pl.BlockSpec((tm, LANES), lambda i: (i, 0)),
                  pl.BlockSpec((tm, LANES), lambda i: (i, 0))],
        out_specs=[pl.BlockSpec((tm, c_conv), lambda i: (i, 0)),
                   pl.BlockSpec((1, N_HEADS, LANES, tm), lambda i: (i // tpb, 0, 0, i % tpb)),
                   pl.BlockSpec((1, N_HEADS, tm, LANES), lambda i: (i // tpb, 0, i % tpb, 0)),
                   pl.BlockSpec((1, c_conv, tm), lambda i: (i // tpb, 0, i % tpb))],
        out_shape=[jax.ShapeDtypeStruct((n, c_conv), BF16),
                   jax.ShapeDtypeStruct((b, N_HEADS, LANES, s), BF16),
                   jax.ShapeDtypeStruct((b, N_HEADS, s, LANES), BF16),
                   jax.ShapeDtypeStruct((b, N_HEADS * V_HEAD_DIM, s), BF16)],
        compiler_params=_cparams(("arbitrary",)),
        name="inproj",
    )(x2, mod3, nmix, win_p, qan, wq_p, kvan, wk_p, wv, qn_p, kn_p, cosf, sinf)


def _conv_kernel(prev_ref, cur_ref, w_ref, b_ref, g_ref, beta_ref, o_ref, buf_ref):
    j = pl.program_id(1)
    ts = cur_ref.shape[1]

    @pl.when(j == 0)
    def _():
        buf_ref[0:CONV_HALO, :] = jnp.zeros((CONV_HALO, buf_ref.shape[1]), F32)

    @pl.when(j > 0)
    def _():
        buf_ref[0:CONV_HALO, :] = prev_ref[0].astype(F32)

    buf_ref[CONV_HALO:CONV_HALO + ts, :] = cur_ref[0].astype(F32)
    first = CONV_HALO - (CONV_K - 1)
    for r0 in range(0, ts, CONV_ROWS):
        acc = buf_ref[r0 + first:r0 + first + CONV_ROWS, :] * w_ref[0:1, :]
        for kk in range(1, CONV_K):
            acc = acc + buf_ref[r0 + first + kk:r0 + first + kk + CONV_ROWS, :] * w_ref[kk:kk + 1, :]
        dw = acc + b_ref[...]
        mu = jnp.mean(dw, axis=-1, keepdims=True)
        cen = dw - mu
        var = jnp.mean(cen * cen, axis=-1, keepdims=True)
        z = cen * lax.rsqrt(var + EPS) * g_ref[...] + beta_ref[...]
        o_ref[0, r0:r0 + CONV_ROWS, :] = (z * _sigmoid(z)).astype(BF16)


def _conv(u3, conv_w_p, conv_b, ln_g, ln_b):
    b, s, c = u3.shape
    ts = TS_CONV
    hb = ts // CONV_HALO
    return pl.pallas_call(
        _conv_kernel,
        grid=(b, s // ts),
        in_specs=[pl.BlockSpec((1, CONV_HALO, c), lambda bi, j: (bi, jnp.maximum(j * hb - 1, 0), 0)),
                  pl.BlockSpec((1, ts, c), lambda bi, j: (bi, j, 0)),
                  pl.BlockSpec(conv_w_p.shape, lambda bi, j: (0, 0)),
                  pl.BlockSpec((1, c), lambda bi, j: (0, 0)),
                  pl.BlockSpec((1, c), lambda bi, j: (0, 0)),
                  pl.BlockSpec((1, c), lambda bi, j: (0, 0))],
        out_specs=pl.BlockSpec((1, ts, c), lambda bi, j: (bi, j, 0)),
        out_shape=jax.ShapeDtypeStruct((b, s, c), BF16),
        scratch_shapes=[pltpu.VMEM((CONV_HALO + ts, c), F32)],
        compiler_params=_cparams(("arbitrary", "arbitrary")),
        name="conv",
    )(u3, u3, conv_w_p, conv_b, ln_g, ln_b)


def _attn_kernel(qt_ref, k_ref, vt_ref, o_ref, s_ref):
    qi = pl.program_id(2)
    tq = qt_ref.shape[3]
    heads = qt_ref.shape[1]
    key_chunk = lax.broadcasted_iota(jnp.int32, (tq, tq), 0) // CHUNK
    qry_chunk = lax.broadcasted_iota(jnp.int32, (tq, tq), 1) // CHUNK
    diag_mask = key_chunk <= qry_chunk

    def scores(jt, slot):
        start = pl.multiple_of(jt * tq, tq)
        for hh in range(heads):
            kj = k_ref[0, hh, pl.ds(start, tq), :]
            s_ref[slot, hh] = jnp.dot(kj, qt_ref[0, hh], preferred_element_type=F32)

    def consume(jt, slot, carry, masked):
        start = pl.multiple_of(jt * tq, tq)
        new = []
        for hh in range(heads):
            m, l, acc = carry[hh]
            st = s_ref[slot, hh]
            if masked:
                st = jnp.where(diag_mask, st, -jnp.inf)
            vj = vt_ref[0, hh * V_HEAD_DIM:(hh + 1) * V_HEAD_DIM, pl.ds(start, tq)]
            m_new = jnp.maximum(m, jnp.max(st, axis=0, keepdims=True))
            alpha = jnp.exp2(m - m_new)
            p = jnp.exp2(st - m_new)
            l_new = alpha * l + jnp.sum(p, axis=0, keepdims=True)
            acc_new = alpha * acc + jnp.dot(vj, p.astype(BF16), preferred_element_type=F32)
            new.append((m_new, l_new, acc_new))
        return tuple(new)

    def body(jt, carry):
        slot = jt % 2
        new = consume(jt, slot, carry, False)
        scores(jt + 1, 1 - slot)
        return new

    init = tuple((jnp.full((1, tq), -jnp.inf, F32), jnp.zeros((1, tq), F32),
                  jnp.zeros((V_HEAD_DIM, tq), F32)) for _ in range(heads))
    scores(0, 0)
    carry = lax.fori_loop(0, qi, body, init)
    carry = consume(qi, qi % 2, carry, True)
    out_t = jnp.concatenate([acc / l for (_, l, acc) in carry], axis=0)
    o_ref[0] = out_t.T.astype(BF16)


def _attn(qt4, k4, vt3):
    b, nh, s, _ = k4.shape
    tq = TQ_ATTN
    hps = HEADS_PER_STEP
    return pl.pallas_call(
        _attn_kernel,
        scratch_shapes=[pltpu.VMEM((2, hps, tq, tq), F32)],
        grid=(b, nh // hps, s // tq),
        in_specs=[pl.BlockSpec((1, hps, LANES, tq), lambda bi, hp, qi: (bi, hp, 0, qi)),
                  pl.BlockSpec((1, hps, s, LANES), lambda bi, hp, qi: (bi, hp, 0, 0)),
                  pl.BlockSpec((1, hps * V_HEAD_DIM, s), lambda bi, hp, qi: (bi, hp, 0))],
        out_specs=pl.BlockSpec((1, tq, hps * V_HEAD_DIM), lambda bi, hp, qi: (bi, qi, hp)),
        out_shape=jax.ShapeDtypeStruct((b, s, nh * V_HEAD_DIM), BF16),
        compiler_params=_cparams(("arbitrary", "arbitrary", "arbitrary")),
        name="attn",
    )(qt4, k4, vt3)


def _outproj_kernel(yc_ref, ya_ref, x_ref, mod_ref, wout_ref, nffn_ref, wr_ref, br_ref,
                    x1_ref, h2_ref, route_ref):
    g_a = mod_ref[0, 2:3, :]
    sh_f = mod_ref[0, 3:4, :]
    sc_f = mod_ref[0, 4:5, :]
    ycat = jnp.concatenate([yc_ref[...], ya_ref[...]], axis=-1)
    mixed = jnp.dot(ycat, wout_ref[...], preferred_element_type=F32)
    x1 = x_ref[...] + g_a * mixed
    x1_ref[...] = x1
    y = x1 * lax.rsqrt(jnp.mean(x1 * x1, axis=-1, keepdims=True) + EPS) * nffn_ref[...]
    h2 = y * (1.0 + sc_f) + sh_f
    h2_ref[...] = h2
    logits = jnp.dot(h2.astype(BF16), wr_ref[...], preferred_element_type=F32) + br_ref[...]
    lane = lax.broadcasted_iota(jnp.int32, logits.shape, 1)
    lane_f = lane.astype(F32)
    big = float(LANES)
    is_g = (lane >= N_EXPERTS) & (lane < N_EXPERTS + N_GROUPS)
    gl = jnp.where(is_g, logits, -jnp.inf)
    gmax = jnp.max(gl, axis=-1, keepdims=True)
    g_top = 1.0 / jnp.sum(jnp.exp(gl - gmax), axis=-1, keepdims=True)
    g_idx = jnp.min(jnp.where(gl == gmax, lane_f - N_EXPERTS, big), axis=-1, keepdims=True)
    in_grp = (lane < N_EXPERTS) & ((lane // EXPERTS_PER_GROUP).astype(F32) == g_idx)
    el = jnp.where(in_grp, logits, -jnp.inf)
    e1 = jnp.max(el, axis=-1, keepdims=True)
    i1 = jnp.min(jnp.where(el == e1, lane_f, big), axis=-1, keepdims=True)
    el2 = jnp.where(lane_f == i1, -jnp.inf, el)
    e2 = jnp.max(el2, axis=-1, keepdims=True)
    i2 = jnp.min(jnp.where(el2 == e2, lane_f, big), axis=-1, keepdims=True)
    t = jnp.exp(e2 - e1)
    w1 = g_top / (1.0 + t)
    w2 = g_top * t / (1.0 + t)
    route_ref[...] = jnp.where(lane == 0, w1, jnp.where(lane == 1, w2, jnp.where(
        lane == 2, i1, jnp.where(lane == 3, i2, 0.0))))


def _outproj(yc, ya, x2, mod3, wout, nffn, wr, br, s):
    n, d = x2.shape
    tm = TM_PROJ
    tpb = s // tm
    c = yc.shape[1]
    full = lambda a: pl.BlockSpec(a.shape, lambda i: (0,) * a.ndim)
    return pl.pallas_call(
        _outproj_kernel,
        grid=(n // tm,),
        in_specs=[pl.BlockSpec((tm, c), lambda i: (i, 0)),
                  pl.BlockSpec((tm, c), lambda i: (i, 0)),
                  pl.BlockSpec((tm, d), lambda i: (i, 0)),
                  pl.BlockSpec((1, 6, d), lambda i: (i // tpb, 0, 0)),
                  full(wout), full(nffn), full(wr), full(br)],
        out_specs=[pl.BlockSpec((tm, d), lambda i: (i, 0)),
                   pl.BlockSpec((tm, d), lambda i: (i, 0)),
                   pl.BlockSpec((tm, LANES), lambda i: (i, 0))],
        out_shape=[jax.ShapeDtypeStruct((n, d), F32),
                   jax.ShapeDtypeStruct((n, d), F32),
                   jax.ShapeDtypeStruct((n, LANES), F32)],
        compiler_params=_cparams(("arbitrary",)),
        name="outproj",
    )(yc, ya, x2, mod3, wout, nffn, wr, br)


def _gather_rows_step(i, n_steps, idx_hbm, src_hbm, ismem, buf, isem, rsem):
    n_rows = ismem.shape[1]
    slot = i % 2
    nslot = 1 - slot

    def idx_copy(step, s):
        return pltpu.make_async_copy(idx_hbm.at[step], ismem.at[s], isem.at[s])

    def row_copy(tok, r, s):
        return pltpu.make_async_copy(src_hbm.at[pl.ds(tok, 1)], buf.at[s, pl.ds(r, 1)], rsem.at[s])

    def issue_rows(s):
        def body(r, carry):
            row_copy(ismem[s, r], r, s).start()
            return carry
        lax.fori_loop(0, n_rows, body, 0, unroll=8)

    def wait_rows(s):
        def body(r, carry):
            row_copy(0, r, s).wait()
            return carry
        lax.fori_loop(0, n_rows, body, 0, unroll=8)

    @pl.when(i == 0)
    def _():
        idx_copy(0, 0).start()
        idx_copy(0, 0).wait()
        issue_rows(0)

        @pl.when(n_steps > 1)
        def _():
            idx_copy(1, 1).start()

    @pl.when(i + 1 < n_steps)
    def _():
        idx_copy(i + 1, nslot).wait()
        issue_rows(nslot)

    @pl.when(i + 2 < n_steps)
    def _():
        idx_copy(i + 2, slot).start()

    wait_rows(slot)


def _moe_kernel(te_ref, nv_ref, rt_hbm, h2_hbm, wg_ref, wu_ref, wd_ref, y_ref,
                ismem, xbuf, isem, rsem):
    i = pl.program_id(0)
    n_valid = nv_ref[0]

    @pl.when(i < n_valid)
    def _():
        _gather_rows_step(i, n_valid, rt_hbm, h2_hbm, ismem, xbuf, isem, rsem)
        xg = xbuf[i % 2].astype(BF16)
        g = jnp.dot(xg, wg_ref[0], preferred_element_type=F32)
        u = jnp.dot(xg, wu_ref[0], preferred_element_type=F32)
        a = (g * _sigmoid(g) * u).astype(BF16)
        y_ref[...] = jnp.dot(a, wd_ref[0], preferred_element_type=F32)

    @pl.when(i >= n_valid)
    def _():
        y_ref[...] = jnp.zeros(y_ref.shape, F32)


def _moe(tile_expert, n_valid, row_token2, h2, wg, wu, wd):
    n_tiles, t = row_token2.shape
    d = h2.shape[1]
    de = wg.shape[2]
    grid_spec = pltpu.PrefetchScalarGridSpec(
        num_scalar_prefetch=2,
        grid=(n_tiles,),
        in_specs=[pl.BlockSpec(memory_space=pl.ANY),
                  pl.BlockSpec(memory_space=pl.ANY),
                  pl.BlockSpec((1, d, de), lambda i, te, nv: (te[i], 0, 0)),
                  pl.BlockSpec((1, d, de), lambda i, te, nv: (te[i], 0, 0)),
                  pl.BlockSpec((1, de, d), lambda i, te, nv: (te[i], 0, 0))],
        out_specs=pl.BlockSpec((t, d), lambda i, te, nv: (i, 0)),
        scratch_shapes=[pltpu.SMEM((2, t), jnp.int32),
                        pltpu.VMEM((2, t, d), F32),
                        pltpu.SemaphoreType.DMA((2,)),
                        pltpu.SemaphoreType.DMA((2,))])
    return pl.pallas_call(
        _moe_kernel,
        grid_spec=grid_spec,
        out_shape=jax.ShapeDtypeStruct((n_tiles * t, d), F32),
        compiler_params=_cparams(("arbitrary",)),
        name="moe",
    )(tile_expert, n_valid, row_token2, h2, wg, wu, wd)


def _combine_kernel(pos_hbm, y_hbm, x1_ref, route_ref, mod_ref, o_ref, ismem, ybuf, isem, rsem):
    i = pl.program_id(0)
    tm = x1_ref.shape[0]
    _gather_rows_step(i, pl.num_programs(0), pos_hbm, y_hbm, ismem, ybuf, isem, rsem)
    slot = i % 2
    g_f = mod_ref[0, 5:6, :]
    w0 = route_ref[:, 0:1]
    w1 = route_ref[:, 1:2]
    y = w0 * ybuf[slot, 0:tm, :] + w1 * ybuf[slot, tm:2 * tm, :]
    o_ref[...] = x1_ref[...] + g_f * y


def _combine(pos2, y_sorted, x1, route, mod3, s):
    n, d = x1.shape
    tm = TM_COMB
    tpb = s // tm
    return pl.pallas_call(
        _combine_kernel,
        grid=(n // tm,),
        in_specs=[pl.BlockSpec(memory_space=pl.ANY),
                  pl.BlockSpec(memory_space=pl.ANY),
                  pl.BlockSpec((tm, d), lambda i: (i, 0)),
                  pl.BlockSpec((tm, LANES), lambda i: (i, 0)),
                  pl.BlockSpec((1, 6, d), lambda i: (i // tpb, 0, 0))],
        out_specs=pl.BlockSpec((tm, d), lambda i: (i, 0)),
        out_shape=jax.ShapeDtypeStruct((n, d), F32),
        scratch_shapes=[pltpu.SMEM((2, 2 * tm), jnp.int32),
                        pltpu.VMEM((2, 2 * tm, d), F32),
                        pltpu.SemaphoreType.DMA((2,)),
                        pltpu.SemaphoreType.DMA((2,))],
        compiler_params=_cparams(("arbitrary",)),
        name="combine",
    )(pos2, y_sorted, x1, route, mod3)


def _routing_tables(route, n):
    t = T_MOE
    flat_e = route[:, 2:4].astype(jnp.int32).reshape(-1)
    onehot = (flat_e[:, None] == jnp.arange(N_EXPERTS, dtype=jnp.int32)[None, :]).astype(jnp.int32)
    csum = jnp.cumsum(onehot, axis=0)
    rank = jnp.sum(onehot * csum, axis=1) - 1
    counts = csum[-1]
    ptiles = (counts + t - 1) // t
    tile_end = jnp.cumsum(ptiles)
    tile_start = tile_end - ptiles
    pos = jnp.sum(onehot * (tile_start * t)[None, :], axis=1) + rank
    n_tiles = (TOP_K * n) // t + N_EXPERTS
    n_valid = tile_end[-1]
    tile_ids = jnp.arange(n_tiles, dtype=jnp.int32)
    te = jnp.sum((tile_ids[:, None] >= tile_end[None, :]).astype(jnp.int32), axis=1)
    te_last = jnp.sum((n_valid - 1 >= tile_end).astype(jnp.int32))
    tile_expert = jnp.minimum(jnp.where(tile_ids < n_valid, te, te_last), N_EXPERTS - 1)
    row_token = jnp.zeros((n_tiles * t,), jnp.int32).at[pos].set(
        jnp.arange(TOP_K * n, dtype=jnp.int32) // TOP_K)
    pos2 = pos.reshape(n // TM_COMB, TM_COMB, TOP_K).transpose(0, 2, 1).reshape(n // TM_COMB, TOP_K * TM_COMB)
    return (tile_expert.astype(jnp.int32), n_valid.astype(jnp.int32).reshape(1),
            row_token.reshape(n_tiles, t), pos2.astype(jnp.int32))


def _pad_lanes(a, left, total):
    pad = [(0, 0)] * (a.ndim - 1) + [(left, total - left - a.shape[-1])]
    return jnp.pad(a, pad)


def _layer(x2, c, cosf, sinf, b, s, w_ada, b_ada, norm_mix, w_in, conv_w, conv_b, conv_ln_g, conv_ln_b,
           q_a_norm, w_q_b, kv_a_norm, w_kv_b, q_norm, k_norm, w_out, norm_ffn, w_group, b_group,
           w_expert, b_expert, w_gate_e, w_up_e, w_down_e):
    n, d = x2.shape
    c_conv = conv_w.shape[1]
    q_rank = q_a_norm.shape[0]
    kv_rank = kv_a_norm.shape[0]
    row = lambda a: a.reshape(1, -1)

    o1 = 2 * c_conv
    o2 = o1 + q_rank
    o3 = o2 + kv_rank
    win_p = jnp.concatenate([w_in[:, :o3], _pad_lanes(w_in[:, o3:], QK_NOPE_DIM, LANES)], axis=1).astype(BF16)
    wq_p = _pad_lanes(w_q_b.reshape(q_rank, N_HEADS, QK_HEAD_DIM), 0, LANES).reshape(q_rank, N_HEADS * LANES)
    wkv = w_kv_b.reshape(kv_rank, N_HEADS, QK_NOPE_DIM + V_HEAD_DIM)
    wk_p = _pad_lanes(wkv[..., :QK_NOPE_DIM], 0, LANES).reshape(kv_rank, N_HEADS * LANES)
    wv = wkv[..., QK_NOPE_DIM:].reshape(kv_rank, N_HEADS * V_HEAD_DIM)
    qn_p = _pad_lanes(row(q_norm), 0, LANES)
    kn_p = _pad_lanes(row(k_norm), 0, LANES)
    wr = _pad_lanes(jnp.concatenate([w_expert, w_group], axis=1), 0, LANES).astype(BF16)
    br = _pad_lanes(row(jnp.concatenate([b_expert, b_group])), 0, LANES)
    conv_w_p = jnp.pad(conv_w, ((0, CONV_HALO - CONV_K), (0, 0)))

    mod3 = _ada(c, w_ada, b_ada).reshape(b, 6, d)
    u, qt4, k4, vt3 = _inproj(x2, mod3, row(norm_mix), win_p, row(q_a_norm), wq_p.astype(BF16),
                           row(kv_a_norm), wk_p.astype(BF16), wv.astype(BF16), qn_p, kn_p,
                           cosf, sinf, b, s)
    y_conv = _conv(u.reshape(b, s, c_conv), conv_w_p, row(conv_b), row(conv_ln_g), row(conv_ln_b))
    y_attn = _attn(qt4, k4, vt3)
    x1, h2, route = _outproj(y_conv.reshape(n, c_conv), y_attn.reshape(n, -1), x2, mod3,
                             w_out.astype(BF16), row(norm_ffn), wr, br, s)
    tile_expert, n_valid, row_token2, pos2 = _routing_tables(route, n)
    y_sorted = _moe(tile_expert, n_valid, row_token2, h2,
                    w_gate_e.astype(BF16), w_up_e.astype(BF16), w_down_e.astype(BF16))
    return _combine(pos2, y_sorted, x1, route, mod3, s)


def kernel(x, c, positions, w_ada, b_ada, norm_mix, w_in, conv_w, conv_b, conv_ln_g, conv_ln_b, q_a_norm,
           w_q_b, kv_a_norm, w_kv_b, q_norm, k_norm, w_out, norm_ffn, w_group, b_group, w_expert, b_expert,
           w_gate_e, w_up_e, w_down_e):
    b, s, d = x.shape
    n = b * s
    half = QK_ROPE_DIM // 2
    inv_freq = ROPE_THETA ** (-jnp.arange(0, QK_ROPE_DIM, 2, dtype=F32) / QK_ROPE_DIM)
    ang = positions.astype(F32).reshape(n, 1) * inv_freq[None, :]
    cos, sin = jnp.cos(ang), jnp.sin(ang)
    cosf = jnp.concatenate([jnp.ones((n, QK_NOPE_DIM), F32), cos, cos,
                            jnp.zeros((n, LANES - QK_HEAD_DIM), F32)], axis=1)
    sinf = jnp.concatenate([jnp.zeros((n, QK_NOPE_DIM), F32), -sin, sin,
                            jnp.zeros((n, LANES - QK_HEAD_DIM), F32)], axis=1)
    x2 = x.reshape(n, d)
    for l in range(w_ada.shape[0]):
        x2 = _layer(x2, c, cosf, sinf, b, s, w_ada[l], b_ada[l], norm_mix[l], w_in[l], conv_w[l], conv_b[l],
                    conv_ln_g[l], conv_ln_b[l], q_a_norm[l], w_q_b[l], kv_a_norm[l], w_kv_b[l], q_norm[l],
                    k_norm[l], w_out[l], norm_ffn[l], w_group[l], b_group[l], w_expert[l], b_expert[l],
                    w_gate_e[l], w_up_e[l], w_down_e[l])
    return x2.reshape(b, s, d)
```

```python
import functools

import jax
import jax.numpy as jnp
from jax import lax
from jax.experimental import pallas as pl
from jax.experimental.pallas import tpu as pltpu

F32 = jnp.float32
BF16 = jnp.bfloat16

CHUNK = 64
CONV_K = 31
N_HEADS = 8
V_HEAD_DIM = 64
QK_NOPE_DIM = 64
QK_ROPE_DIM = 32
QK_HEAD_DIM = QK_NOPE_DIM + QK_ROPE_DIM
N_GROUPS = 4
EXPERTS_PER_GROUP = 8
N_EXPERTS = N_GROUPS * EXPERTS_PER_GROUP
TOP_K = 2
ROPE_THETA = 10000.0
EPS = 1e-6
LOG2_E = 1.4426950408889634

LANES = 128
SUBLANES = 8
CONV_HALO = 32

TM_PROJ = 512
TS_CONV = 256
CONV_ROWS = 32
TQ_ATTN = 256
HEADS_PER_STEP = 4
T_MOE = 256
IDX_SLOT = 1024
TB_DISPATCH = IDX_SLOT // TOP_K
TM_COMB = IDX_SLOT // TOP_K
VMEM_LIMIT = 48 * 1024 * 1024


def _sigmoid(v):
    return 1.0 / (1.0 + jnp.exp(-v))


def _cparams(sem):
    return pltpu.CompilerParams(dimension_semantics=sem, vmem_limit_bytes=VMEM_LIMIT)


def _ada_kernel(c_ref, w_ref, b_ref, o_ref):
    c = c_ref[...]
    s = c * _sigmoid(c)
    o_ref[...] = jnp.dot(s, w_ref[...], preferred_element_type=F32,
                         precision=lax.Precision.HIGHEST) + b_ref[...]


def _ada(c, w_ada, b_ada):
    b, d = c.shape
    n_out = w_ada.shape[1]
    return pl.pallas_call(
        _ada_kernel,
        grid=(n_out // d,),
        in_specs=[pl.BlockSpec((b, d), lambda j: (0, 0)),
                  pl.BlockSpec((d, d), lambda j: (0, j)),
                  pl.BlockSpec((1, d), lambda j: (0, j))],
        out_specs=pl.BlockSpec((b, d), lambda j: (0, j)),
        out_shape=jax.ShapeDtypeStruct((b, n_out), F32),
        compiler_params=_cparams(("arbitrary",)),
        name="ada",
    )(c, w_ada, b_ada.reshape(1, n_out))


def _head_norm_rope(t, gain, cosf, sinf, lane):
    r = lax.rsqrt(jnp.sum(t * t, axis=-1, keepdims=True) * (1.0 / QK_HEAD_DIM) + EPS)
    tn = t * r * gain
    half = QK_ROPE_DIM // 2
    swapped = jnp.where(lane < QK_NOPE_DIM + half,
                        pltpu.roll(tn, LANES - half, 1), pltpu.roll(tn, half, 1))
    return tn * cosf + swapped * sinf


def _inproj_kernel(x_ref, mod_ref, nmix_ref, win_ref, qan_ref, wq_ref, kvan_ref, wk_ref, wv_ref,
                   qn_ref, kn_ref, cos_ref, sin_ref, u_ref, qt_ref, k_ref, vt_ref):
    x = x_ref[...]
    sh_a = mod_ref[0, 0:1, :]
    sc_a = mod_ref[0, 1:2, :]
    y = x * lax.rsqrt(jnp.mean(x * x, axis=-1, keepdims=True) + EPS) * nmix_ref[...]
    h = y * (1.0 + sc_a) + sh_a
    proj = jnp.dot(h.astype(BF16), win_ref[...], preferred_element_type=F32)
    c_conv = u_ref.shape[-1]
    val = proj[:, :c_conv]
    gate = proj[:, c_conv:2 * c_conv]
    u_ref[...] = (val * _sigmoid(gate)).astype(BF16)
    o = 2 * c_conv
    q_rank = wq_ref.shape[0]
    kv_rank = wk_ref.shape[0]
    cq = proj[:, o:o + q_rank]
    ckv = proj[:, o + q_rank:o + q_rank + kv_rank]
    kr = proj[:, o + q_rank + kv_rank:o + q_rank + kv_rank + LANES]
    cqn = cq * lax.rsqrt(jnp.mean(cq * cq, axis=-1, keepdims=True) + EPS) * qan_ref[...]
    ckvn = ckv * lax.rsqrt(jnp.mean(ckv * ckv, axis=-1, keepdims=True) + EPS) * kvan_ref[...]
    q_all = jnp.dot(cqn.astype(BF16), wq_ref[...], preferred_element_type=F32)
    ckvn_b = ckvn.astype(BF16)
    k_all = jnp.dot(ckvn_b, wk_ref[...], preferred_element_type=F32)
    vt_ref[0] = jnp.dot(ckvn_b, wv_ref[...], preferred_element_type=F32).T.astype(BF16)
    cosf = cos_ref[...]
    sinf = sin_ref[...]
    lane = lax.broadcasted_iota(jnp.int32, cosf.shape, 1)
    scale = QK_HEAD_DIM ** -0.5 * LOG2_E
    for hd in range(N_HEADS):
        qh = q_all[:, hd * LANES:(hd + 1) * LANES]
        qt_ref[0, hd] = (_head_norm_rope(qh, qn_ref[...], cosf, sinf, lane) * scale).T.astype(BF16)
        kh = k_all[:, hd * LANES:(hd + 1) * LANES] + kr
        k_ref[0, hd] = _head_norm_rope(kh, kn_ref[...], cosf, sinf, lane).astype(BF16)


def _inproj(x2, mod3, nmix, win_p, qan, wq_p, kvan, wk_p, wv, qn_p, kn_p, cosf, sinf, b, s):
    n, d = x2.shape
    tm = TM_PROJ
    tpb = s // tm
    c_conv = (win_p.shape[1] - wq_p.shape[0] - wk_p.shape[0] - LANES) // 2
    full = lambda a: pl.BlockSpec(a.shape, lambda i: (0,) * a.ndim)
    return pl.pallas_call(
        _inproj_kernel,
        grid=(n // tm,),
        in_specs=[pl.BlockSpec((tm, d), lambda i: (i, 0)),
                  pl.BlockSpec((1, 6, d), lambda i: (i // tpb, 0, 0)),
                  full(nmix), full(win_p), full(qan), full(wq_p), full(kvan), full(wk_p), full(wv),
                  full(qn_p), full(kn_p),
                  pl.BlockSpec((tm, LANES), lambda i: (i, 0)),
                  pl.BlockSpec((tm, LANES), lambda i: (i, 0))],
        out_specs=[pl.BlockSpec((tm, c_conv), lambda i: (i, 0)),
                   pl.BlockSpec((1, N_HEADS, LANES, tm), lambda i: (i // tpb, 0, 0, i % tpb)),
                   pl.BlockSpec((1, N_HEADS, tm, LANES), lambda i: (i // tpb, 0, i % tpb, 0)),
                   pl.BlockSpec((1, c_conv, tm), lambda i: (i // tpb, 0, i % tpb))],
        out_shape=[jax.ShapeDtypeStruct((n, c_conv), BF16),
                   jax.ShapeDtypeStruct((b, N_HEADS, LANES, s), BF16),
                   jax.ShapeDtypeStruct((b, N_HEADS, s, LANES), BF16),
                   jax.ShapeDtypeStruct((b, N_HEADS * V_HEAD_DIM, s), BF16)],
        compiler_params=_cparams(("arbitrary",)),
        name="inproj",
    )(x2, mod3, nmix, win_p, qan, wq_p, kvan, wk_p, wv, qn_p, kn_p, cosf, sinf)


def _conv_kernel(prev_ref, cur_ref, w_ref, b_ref, g_ref, beta_ref, o_ref, buf_ref, shift_ref):
    j = pl.program_id(1)
    ts = cur_ref.shape[1]

    @pl.when(j == 0)
    def _():
        buf_ref[0:CONV_HALO, :] = jnp.zeros((CONV_HALO, buf_ref.shape[1]), F32)

    @pl.when(j > 0)
    def _():
        buf_ref[0:CONV_HALO, :] = prev_ref[0].astype(F32)

    buf_ref[CONV_HALO:CONV_HALO + ts, :] = cur_ref[0].astype(F32)
    span = CONV_HALO + ts - SUBLANES
    for sh in range(1, SUBLANES):
        shift_ref[sh - 1, 0:span, :] = buf_ref[sh:sh + span, :]
    first = CONV_HALO - (CONV_K - 1)
    for r0 in range(0, ts, CONV_ROWS):
        acc = None
        for kk in range(CONV_K):
            sh = (first + kk) % SUBLANES
            base = r0 + (first + kk) - sh
            if sh == 0:
                win = buf_ref[base:base + CONV_ROWS, :]
            else:
                win = shift_ref[sh - 1, base:base + CONV_ROWS, :]
            term = win * w_ref[kk:kk + 1, :]
            acc = term if acc is None else acc + term
        dw = acc + b_ref[...]
        mu = jnp.mean(dw, axis=-1, keepdims=True)
        cen = dw - mu
        var = jnp.mean(cen * cen, axis=-1, keepdims=True)
        z = cen * lax.rsqrt(var + EPS) * g_ref[...] + beta_ref[...]
        o_ref[0, r0:r0 + CONV_ROWS, :] = (z * _sigmoid(z)).astype(BF16)


def _conv(u3, conv_w_p, conv_b, ln_g, ln_b):
    b, s, c = u3.shape
    ts = TS_CONV
    hb = ts // CONV_HALO
    return pl.pallas_call(
        _conv_kernel,
        grid=(b, s // ts),
        in_specs=[pl.BlockSpec((1, CONV_HALO, c), lambda bi, j: (bi, jnp.maximum(j * hb - 1, 0), 0)),
                  pl.BlockSpec((1, ts, c), lambda bi, j: (bi, j, 0)),
                  pl.BlockSpec(conv_w_p.shape, lambda bi, j: (0, 0)),
                  pl.BlockSpec((1, c), lambda bi, j: (0, 0)),
                  pl.BlockSpec((1, c), lambda bi, j: (0, 0)),
                  pl.BlockSpec((1, c), lambda bi, j: (0, 0))],
        out_specs=pl.BlockSpec((1, ts, c), lambda bi, j: (bi, j, 0)),
        out_shape=jax.ShapeDtypeStruct((b, s, c), BF16),
        scratch_shapes=[pltpu.VMEM((CONV_HALO + ts, c), F32),
                        pltpu.VMEM((SUBLANES - 1, CONV_HALO + ts - SUBLANES, c), F32)],
        compiler_params=_cparams(("arbitrary", "arbitrary")),
        name="conv",
    )(u3, u3, conv_w_p, conv_b, ln_g, ln_b)


def _attn_kernel(qt_ref, k_ref, vt_ref, o_ref, s_ref):
    qi = pl.program_id(2)
    tq = qt_ref.shape[3]
    heads = qt_ref.shape[1]
    key_chunk = lax.broadcasted_iota(jnp.int32, (tq, tq), 0) // CHUNK
    qry_chunk = lax.broadcasted_iota(jnp.int32, (tq, tq), 1) // CHUNK
    diag_mask = key_chunk <= qry_chunk

    def scores(jt, slot):
        start = pl.multiple_of(jt * tq, tq)
        for hh in range(heads):
            kj = k_ref[0, hh, pl.ds(start, tq), :]
            s_ref[slot, hh] = jnp.dot(kj, qt_ref[0, hh], preferred_element_type=F32)

    def consume(jt, slot, carry, masked):
        start = pl.multiple_of(jt * tq, tq)
        new = []
        for hh in range(heads):
            m, l, acc = carry[hh]
            st = s_ref[slot, hh]
            if masked:
                st = jnp.where(diag_mask, st, -jnp.inf)
            vj = vt_ref[0, hh * V_HEAD_DIM:(hh + 1) * V_HEAD_DIM, pl.ds(start, tq)]
            m_new = jnp.maximum(m, jnp.max(st, axis=0, keepdims=True))
            alpha = jnp.exp2(m - m_new)
            p = jnp.exp2(st - m_new)
            l_new = alpha * l + jnp.sum(p, axis=0, keepdims=True)
            acc_new = alpha * acc + jnp.dot(vj, p.astype(BF16), preferred_element_type=F32)
            new.append((m_new, l_new, acc_new))
        return tuple(new)

    def body(jt, carry):
        slot = jt % 2
        new = consume(jt, slot, carry, False)
        scores(jt + 1, 1 - slot)
        return new

    init = tuple((jnp.full((1, tq), -jnp.inf, F32), jnp.zeros((1, tq), F32),
                  jnp.zeros((V_HEAD_DIM, tq), F32)) for _ in range(heads))
    scores(0, 0)
    carry = lax.fori_loop(0, qi, body, init)
    carry = consume(qi, qi % 2, carry, True)
    out_t = jnp.concatenate([acc / l for (_, l, acc) in carry], axis=0)
    o_ref[0] = out_t.T.astype(BF16)


def _attn(qt4, k4, vt3):
    b, nh, s, _ = k4.shape
    tq = TQ_ATTN
    hps = HEADS_PER_STEP
    return pl.pallas_call(
        _attn_kernel,
        scratch_shapes=[pltpu.VMEM((2, hps, tq, tq), F32)],
        grid=(b, nh // hps, s // tq),
        in_specs=[pl.BlockSpec((1, hps, LANES, tq), lambda bi, hp, qi: (bi, hp, 0, qi)),
                  pl.BlockSpec((1, hps, s, LANES), lambda bi, hp, qi: (bi, hp, 0, 0)),
                  pl.BlockSpec((1, hps * V_HEAD_DIM, s), lambda bi, hp, qi: (bi, hp, 0))],
        out_specs=pl.BlockSpec((1, tq, hps * V_HEAD_DIM), lambda bi, hp, qi: (bi, qi, hp)),
        out_shape=jax.ShapeDtypeStruct((b, s, nh * V_HEAD_DIM), BF16),
        compiler_params=_cparams(("arbitrary", "arbitrary", "arbitrary")),
        name="attn",
    )(qt4, k4, vt3)


def _ffn_input(x1, nffn, mod_ref):
    sh_f = mod_ref[0, 3:4, :]
    sc_f = mod_ref[0, 4:5, :]
    y = x1 * lax.rsqrt(jnp.mean(x1 * x1, axis=-1, keepdims=True) + EPS) * nffn
    return y * (1.0 + sc_f) + sh_f


def _outproj_kernel(yc_ref, ya_ref, x_ref, mod_ref, wout_ref, nffn_ref, wr_ref, br_ref,
                    x1_ref, route_ref):
    g_a = mod_ref[0, 2:3, :]
    ycat = jnp.concatenate([yc_ref[...], ya_ref[...]], axis=-1)
    mixed = jnp.dot(ycat, wout_ref[...], preferred_element_type=F32)
    x1 = x_ref[...] + g_a * mixed
    x1_ref[...] = x1
    h2 = _ffn_input(x1, nffn_ref[...], mod_ref)
    logits = jnp.dot(h2.astype(BF16), wr_ref[...], preferred_element_type=F32) + br_ref[...]
    lane = lax.broadcasted_iota(jnp.int32, logits.shape, 1)
    lane_f = lane.astype(F32)
    big = float(LANES)
    is_g = (lane >= N_EXPERTS) & (lane < N_EXPERTS + N_GROUPS)
    gl = jnp.where(is_g, logits, -jnp.inf)
    gmax = jnp.max(gl, axis=-1, keepdims=True)
    g_top = 1.0 / jnp.sum(jnp.exp(gl - gmax), axis=-1, keepdims=True)
    g_idx = jnp.min(jnp.where(gl == gmax, lane_f - N_EXPERTS, big), axis=-1, keepdims=True)
    in_grp = (lane < N_EXPERTS) & ((lane // EXPERTS_PER_GROUP).astype(F32) == g_idx)
    el = jnp.where(in_grp, logits, -jnp.inf)
    e1 = jnp.max(el, axis=-1, keepdims=True)
    i1 = jnp.min(jnp.where(el == e1, lane_f, big), axis=-1, keepdims=True)
    el2 = jnp.where(lane_f == i1, -jnp.inf, el)
    e2 = jnp.max(el2, axis=-1, keepdims=True)
    i2 = jnp.min(jnp.where(el2 == e2, lane_f, big), axis=-1, keepdims=True)
    t = jnp.exp(e2 - e1)
    w1 = g_top / (1.0 + t)
    w2 = g_top * t / (1.0 + t)
    route_ref[...] = jnp.where(lane == 0, w1, jnp.where(lane == 1, w2, jnp.where(
        lane == 2, i1, jnp.where(lane == 3, i2, 0.0))))


def _outproj(yc, ya, x2, mod3, wout, nffn, wr, br, s):
    n, d = x2.shape
    tm = TM_PROJ
    tpb = s // tm
    c = yc.shape[1]
    full = lambda a: pl.BlockSpec(a.shape, lambda i: (0,) * a.ndim)
    return pl.pallas_call(
        _outproj_kernel,
        grid=(n // tm,),
        in_specs=[pl.BlockSpec((tm, c), lambda i: (i, 0)),
                  pl.BlockSpec((tm, c), lambda i: (i, 0)),
                  pl.BlockSpec((tm, d), lambda i: (i, 0)),
                  pl.BlockSpec((1, 6, d), lambda i: (i // tpb, 0, 0)),
                  full(wout), full(nffn), full(wr), full(br)],
        out_specs=[pl.BlockSpec((tm, d), lambda i: (i, 0)),
                   pl.BlockSpec((tm, LANES), lambda i: (i, 0))],
        out_shape=[jax.ShapeDtypeStruct((n, d), F32),
                   jax.ShapeDtypeStruct((n, LANES), F32)],
        compiler_params=_cparams(("arbitrary",)),
        name="outproj",
    )(yc, ya, x2, mod3, wout, nffn, wr, br)


def _store_token_tiles(ref, row0, val, lead=()):
    rows = val.shape[0]
    for cc in range(SUBLANES):
        ref[lead + (pl.ds(row0 * SUBLANES + cc, rows, stride=SUBLANES), slice(None))] = (
            val[:, cc * LANES:(cc + 1) * LANES])


def _load_token_tiles(ref, row0, rows, lead=()):
    parts = [ref[lead + (pl.ds(row0 * SUBLANES + cc, rows, stride=SUBLANES), slice(None))]
             for cc in range(SUBLANES)]
    return jnp.concatenate(parts, axis=-1)


def _idx_slot(ismem, s):
    return ismem.at[pl.ds(pl.multiple_of(s * IDX_SLOT, IDX_SLOT), IDX_SLOT)]


def _gather_rows_step(i, n_steps, idx_hbm, src_hbm, ismem, buf, isem, rsem):
    n_rows = IDX_SLOT
    slot = i % 2
    nslot = 1 - slot

    def idx_copy(step, s):
        return pltpu.make_async_copy(idx_hbm.at[step], _idx_slot(ismem, s), isem.at[s])

    def row_copy(row8, r, s):
        return pltpu.make_async_copy(src_hbm.at[pl.ds(pl.multiple_of(row8, SUBLANES), SUBLANES)],
                                     buf.at[s, pl.ds(pl.multiple_of(r * SUBLANES, SUBLANES), SUBLANES)],
                                     rsem.at[s])

    def issue_rows(s):
        def body(r, carry):
            row_copy(ismem[s * IDX_SLOT + r], r, s).start()
            return carry
        lax.fori_loop(0, n_rows, body, 0, unroll=8)

    def wait_rows(s):
        def body(r, carry):
            row_copy(0, r, s).wait()
            return carry
        lax.fori_loop(0, n_rows, body, 0, unroll=8)

    @pl.when(i == 0)
    def _():
        idx_copy(0, 0).start()
        idx_copy(0, 0).wait()
        issue_rows(0)

        @pl.when(n_steps > 1)
        def _():
            idx_copy(1, 1).start()

    @pl.when(i + 1 < n_steps)
    def _():
        idx_copy(i + 1, nslot).wait()
        issue_rows(nslot)

    @pl.when(i + 2 < n_steps)
    def _():
        idx_copy(i + 2, slot).start()

    wait_rows(slot)


def _dispatch_kernel(pad_row_ref, pad_cnt_ref, nv_ref, pos_hbm, x1_ref, mod_ref, nffn_ref, xs_hbm,
                     ismem, hbuf, zero_ref, isem, csem, zsem):
    i = pl.program_id(0)
    n_steps = pl.num_programs(0)
    n_pairs = IDX_SLOT
    tb = n_pairs // TOP_K
    slot = i % 2

    def idx_copy(step, s):
        return pltpu.make_async_copy(pos_hbm.at[step], _idx_slot(ismem, s), isem.at[s])

    def tile_copy(r, row8, s):
        return pltpu.make_async_copy(
            hbuf.at[s, pl.ds(pl.multiple_of(r * SUBLANES, SUBLANES), SUBLANES)],
            xs_hbm.at[pl.ds(pl.multiple_of(row8, SUBLANES), SUBLANES)], csem.at[s])

    def zero_copy(row8):
        return pltpu.make_async_copy(
            zero_ref.at[pl.ds(0, SUBLANES)],
            xs_hbm.at[pl.ds(pl.multiple_of(row8, SUBLANES), SUBLANES)], zsem.at[0])

    def for_pad_tiles(fn):
        def per_expert(e, carry):
            def per_tile(j, c2):
                fn(pad_row_ref[e] + j * SUBLANES)
                return c2
            return lax.fori_loop(0, pad_cnt_ref[e], per_tile, carry)
        lax.fori_loop(0, N_EXPERTS, per_expert, 0)

    tile_rows8 = zero_ref.shape[0]
    n_tiles = xs_hbm.shape[0] // tile_rows8

    def tail_copy(j):
        return pltpu.make_async_copy(
            zero_ref, xs_hbm.at[pl.ds(pl.multiple_of(j * tile_rows8, tile_rows8), tile_rows8)], zsem.at[1])

    def for_tail_tiles(fn):
        def body(j, carry):
            fn(j)
            return carry
        lax.fori_loop(nv_ref[0], n_tiles, body, 0)

    @pl.when(i == 0)
    def _():
        idx_copy(0, 0).start()
        zero_ref[...] = jnp.zeros(zero_ref.shape, F32)
        for_pad_tiles(lambda row8: zero_copy(row8).start())
        for_tail_tiles(lambda j: tail_copy(j).start())

    idx_copy(i, slot).wait()

    @pl.when(i + 1 < n_steps)
    def _():
        idx_copy(i + 1, 1 - slot).start()

    _store_token_tiles(hbuf, 0, _ffn_input(x1_ref[...], nffn_ref[...], mod_ref), (slot,))

    def issue(r, carry):
        for kk in range(TOP_K):
            tile_copy(r, ismem[slot * IDX_SLOT + kk * tb + r], slot).start()
        return carry
    lax.fori_loop(0, tb, issue, 0, unroll=4)

    def drain(s):
        def body(r, carry):
            tile_copy(0, 0, s).wait()
            return carry
        lax.fori_loop(0, n_pairs, body, 0, unroll=8)

    @pl.when(i > 0)
    def _():
        drain(1 - slot)

    @pl.when(i == n_steps - 1)
    def _():
        drain(slot)
        for_pad_tiles(lambda row8: zero_copy(0).wait())
        for_tail_tiles(lambda j: tail_copy(0).wait())


def _dispatch(pad_row8, pad_cnt, n_valid, pos8_blocks, x1, mod3, nffn, n_rows, s):
    n_blocks, n_pairs = pos8_blocks.shape
    n, d = x1.shape
    tb = n_pairs // TOP_K
    tpb = s // tb
    grid_spec = pltpu.PrefetchScalarGridSpec(
        num_scalar_prefetch=3,
        grid=(n_blocks,),
        in_specs=[pl.BlockSpec(memory_space=pl.ANY),
                  pl.BlockSpec((tb, d), lambda i, pr, pc, nv: (i, 0)),
                  pl.BlockSpec((1, 6, d), lambda i, pr, pc, nv: (i // tpb, 0, 0)),
                  pl.BlockSpec((1, d), lambda i, pr, pc, nv: (0, 0))],
        out_specs=pl.BlockSpec(memory_space=pl.ANY),
        scratch_shapes=[pltpu.SMEM((2 * IDX_SLOT,), jnp.int32),
                        pltpu.VMEM((2, tb * SUBLANES, LANES), F32),
                        pltpu.VMEM((T_MOE * SUBLANES, LANES), F32),
                        pltpu.SemaphoreType.DMA((2,)),
                        pltpu.SemaphoreType.DMA((2,)),
                        pltpu.SemaphoreType.DMA((2,))])
    return pl.pallas_call(
        _dispatch_kernel,
        grid_spec=grid_spec,
        out_shape=jax.ShapeDtypeStruct((n_rows * SUBLANES, LANES), F32),
        compiler_params=_cparams(("arbitrary",)),
        name="dispatch",
    )(pad_row8, pad_cnt, n_valid, pos8_blocks, x1, mod3, nffn)


def _moe_kernel(te_ref, nv_ref, xs_ref, wg_ref, wu_ref, wd_ref, y_ref):
    i = pl.program_id(0)
    t = xs_ref.shape[0] // SUBLANES

    @pl.when(i < nv_ref[0])
    def _():
        xg = _load_token_tiles(xs_ref, 0, t).astype(BF16)
        g = jnp.dot(xg, wg_ref[0], preferred_element_type=F32)
        u = jnp.dot(xg, wu_ref[0], preferred_element_type=F32)
        a = (g * _sigmoid(g) * u).astype(BF16)
        _store_token_tiles(y_ref, 0, jnp.dot(a, wd_ref[0], preferred_element_type=F32))

    @pl.when(i >= nv_ref[0])
    def _():
        y_ref[...] = jnp.zeros(y_ref.shape, F32)


def _moe(tile_expert, n_valid, xs, wg, wu, wd):
    t = T_MOE
    n_tiles = xs.shape[0] // (t * SUBLANES)
    d = wg.shape[1]
    de = wg.shape[2]
    grid_spec = pltpu.PrefetchScalarGridSpec(
        num_scalar_prefetch=2,
        grid=(n_tiles,),
        in_specs=[pl.BlockSpec((t * SUBLANES, LANES), lambda i, te, nv: (jnp.minimum(i, nv[0] - 1), 0)),
                  pl.BlockSpec((1, d, de), lambda i, te, nv: (te[i], 0, 0)),
                  pl.BlockSpec((1, d, de), lambda i, te, nv: (te[i], 0, 0)),
                  pl.BlockSpec((1, de, d), lambda i, te, nv: (te[i], 0, 0))],
        out_specs=pl.BlockSpec((t * SUBLANES, LANES), lambda i, te, nv: (i, 0)))
    return pl.pallas_call(
        _moe_kernel,
        grid_spec=grid_spec,
        out_shape=jax.ShapeDtypeStruct(xs.shape, F32),
        compiler_params=_cparams(("arbitrary",)),
        name="moe",
    )(tile_expert, n_valid, xs, wg, wu, wd)


def _combine_kernel(pos_hbm, y_hbm, x1_ref, route_ref, mod_ref, o_ref, ismem, ybuf, isem, rsem):
    i = pl.program_id(0)
    tm = x1_ref.shape[0]
    _gather_rows_step(i, pl.num_programs(0), pos_hbm, y_hbm, ismem, ybuf, isem, rsem)
    slot = i % 2
    g_f = mod_ref[0, 5:6, :]
    w0 = route_ref[:, 0:1]
    w1 = route_ref[:, 1:2]
    y = (w0 * _load_token_tiles(ybuf, 0, tm, (slot,)) + w1 * _load_token_tiles(ybuf, tm, tm, (slot,)))
    o_ref[...] = x1_ref[...] + g_f * y


def _combine(pos2, y_sorted, x1, route, mod3, s):
    n, d = x1.shape
    tm = TM_COMB
    tpb = s // tm
    return pl.pallas_call(
        _combine_kernel,
        grid=(n // tm,),
        in_specs=[pl.BlockSpec(memory_space=pl.ANY),
                  pl.BlockSpec(memory_space=pl.ANY),
                  pl.BlockSpec((tm, d), lambda i: (i, 0)),
                  pl.BlockSpec((tm, LANES), lambda i: (i, 0)),
                  pl.BlockSpec((1, 6, d), lambda i: (i // tpb, 0, 0))],
        out_specs=pl.BlockSpec((tm, d), lambda i: (i, 0)),
        out_shape=jax.ShapeDtypeStruct((n, d), F32),
        scratch_shapes=[pltpu.SMEM((2 * IDX_SLOT,), jnp.int32),
                        pltpu.VMEM((2, TOP_K * tm * SUBLANES, LANES), F32),
                        pltpu.SemaphoreType.DMA((2,)),
                        pltpu.SemaphoreType.DMA((2,))],
        compiler_params=_cparams(("arbitrary",)),
        name="combine",
    )(pos2, y_sorted, x1, route, mod3)


def _routing_tables(route, n):
    t = T_MOE
    flat_e = route[:, 2:4].astype(jnp.int32).reshape(-1)
    onehot = (flat_e[:, None] == jnp.arange(N_EXPERTS, dtype=jnp.int32)[None, :]).astype(jnp.int32)
    csum = jnp.cumsum(onehot, axis=0)
    rank = jnp.sum(onehot * csum, axis=1) - 1
    counts = csum[-1]
    ptiles = (counts + t - 1) // t
    tile_end = jnp.cumsum(ptiles)
    tile_start = tile_end - ptiles
    pos = jnp.sum(onehot * (tile_start * t)[None, :], axis=1) + rank
    n_tiles = (TOP_K * n) // t + N_EXPERTS
    n_valid = tile_end[-1]
    tile_ids = jnp.arange(n_tiles, dtype=jnp.int32)
    te = jnp.sum((tile_ids[:, None] >= tile_end[None, :]).astype(jnp.int32), axis=1)
    te_last = jnp.sum((n_valid - 1 >= tile_end).astype(jnp.int32))
    tile_expert = jnp.minimum(jnp.where(tile_ids < n_valid, te, te_last), N_EXPERTS - 1)
    pos8 = (pos * SUBLANES).astype(jnp.int32).reshape(n, TOP_K)
    blocks = lambda tb: pos8.reshape(n // tb, tb, TOP_K).transpose(0, 2, 1).reshape(n // tb, TOP_K * tb)
    pad_row8 = ((tile_start * t + counts) * SUBLANES).astype(jnp.int32)
    pad_cnt = (ptiles * t - counts).astype(jnp.int32)
    return (tile_expert.astype(jnp.int32), n_valid.astype(jnp.int32).reshape(1), pad_row8, pad_cnt,
            blocks(TB_DISPATCH), blocks(TM_COMB), n_tiles * t)


def _pad_lanes(a, left, total):
    pad = [(0, 0)] * (a.ndim - 1) + [(left, total - left - a.shape[-1])]
    return jnp.pad(a, pad)


def _layer(x2, c, cosf, sinf, b, s, w_ada, b_ada, norm_mix, w_in, conv_w, conv_b, conv_ln_g, conv_ln_b,
           q_a_norm, w_q_b, kv_a_norm, w_kv_b, q_norm, k_norm, w_out, norm_ffn, w_group, b_group,
           w_expert, b_expert, w_gate_e, w_up_e, w_down_e):
    n, d = x2.shape
    c_conv = conv_w.shape[1]
    q_rank = q_a_norm.shape[0]
    kv_rank = kv_a_norm.shape[0]
    row = lambda a: a.reshape(1, -1)

    o1 = 2 * c_conv
    o2 = o1 + q_rank
    o3 = o2 + kv_rank
    win_p = jnp.concatenate([w_in[:, :o3], _pad_lanes(w_in[:, o3:], QK_NOPE_DIM, LANES)], axis=1).astype(BF16)
    wq_p = _pad_lanes(w_q_b.reshape(q_rank, N_HEADS, QK_HEAD_DIM), 0, LANES).reshape(q_rank, N_HEADS * LANES)
    wkv = w_kv_b.reshape(kv_rank, N_HEADS, QK_NOPE_DIM + V_HEAD_DIM)
    wk_p = _pad_lanes(wkv[..., :QK_NOPE_DIM], 0, LANES).reshape(kv_rank, N_HEADS * LANES)
    wv = wkv[..., QK_NOPE_DIM:].reshape(kv_rank, N_HEADS * V_HEAD_DIM)
    qn_p = _pad_lanes(row(q_norm), 0, LANES)
    kn_p = _pad_lanes(row(k_norm), 0, LANES)
    wr = _pad_lanes(jnp.concatenate([w_expert, w_group], axis=1), 0, LANES).astype(BF16)
    br = _pad_lanes(row(jnp.concatenate([b_expert, b_group])), 0, LANES)
    conv_w_p = jnp.pad(conv_w, ((0, CONV_HALO - CONV_K), (0, 0)))

    mod3 = _ada(c, w_ada, b_ada).reshape(b, 6, d)
    u, qt4, k4, vt3 = _inproj(x2, mod3, row(norm_mix), win_p, row(q_a_norm), wq_p.astype(BF16),
                           row(kv_a_norm), wk_p.astype(BF16), wv.astype(BF16), qn_p, kn_p,
                           cosf, sinf, b, s)
    y_conv = _conv(u.reshape(b, s, c_conv), conv_w_p, row(conv_b), row(conv_ln_g), row(conv_ln_b))
    y_attn = _attn(qt4, k4, vt3)
    x1, route = _outproj(y_conv.reshape(n, c_conv), y_attn.reshape(n, -1), x2, mod3,
                         w_out.astype(BF16), row(norm_ffn), wr, br, s)
    tile_expert, n_valid, pad_row8, pad_cnt, pos8_disp, pos8_comb, n_rows = _routing_tables(route, n)
    xs = _dispatch(pad_row8, pad_cnt, n_valid, pos8_disp, x1, mod3, row(norm_ffn), n_rows, s)
    ys = _moe(tile_expert, n_valid, xs, w_gate_e.astype(BF16), w_up_e.astype(BF16), w_down_e.astype(BF16))
    return _combine(pos8_comb, ys, x1, route, mod3, s)


def kernel(x, c, positions, w_ada, b_ada, norm_mix, w_in, conv_w, conv_b, conv_ln_g, conv_ln_b, q_a_norm,
           w_q_b, kv_a_norm, w_kv_b, q_norm, k_norm, w_out, norm_ffn, w_group, b_group, w_expert, b_expert,
           w_gate_e, w_up_e, w_down_e):
    b, s, d = x.shape
    n = b * s
    half = QK_ROPE_DIM // 2
    inv_freq = ROPE_THETA ** (-jnp.arange(0, QK_ROPE_DIM, 2, dtype=F32) / QK_ROPE_DIM)
    ang = positions.astype(F32).reshape(n, 1) * inv_freq[None, :]
    cos, sin = jnp.cos(ang), jnp.sin(ang)
    cosf = jnp.concatenate([jnp.ones((n, QK_NOPE_DIM), F32), cos, cos,
                            jnp.zeros((n, LANES - QK_HEAD_DIM), F32)], axis=1)
    sinf = jnp.concatenate([jnp.zeros((n, QK_NOPE_DIM), F32), -sin, sin,
                            jnp.zeros((n, LANES - QK_HEAD_DIM), F32)], axis=1)
    x2 = x.reshape(n, d)
    for l in range(w_ada.shape[0]):
        x2 = _layer(x2, c, cosf, sinf, b, s, w_ada[l], b_ada[l], norm_mix[l], w_in[l], conv_w[l], conv_b[l],
                    conv_ln_g[l], conv_ln_b[l], q_a_norm[l], w_q_b[l], kv_a_norm[l], w_kv_b[l], q_norm[l],
                    k_norm[l], w_out[l], norm_ffn[l], w_group[l], b_group[l], w_expert[l], b_expert[l],
                    w_gate_e[l], w_up_e[l], w_down_e[l])
    return x2.reshape(b, s, d)
```

```python
import functools

import jax
import jax.numpy as jnp
from jax import lax
from jax.experimental import pallas as pl
from jax.experimental.pallas import tpu as pltpu

F32 = jnp.float32
BF16 = jnp.bfloat16

CHUNK = 64
CONV_K = 31
N_HEADS = 8
V_HEAD_DIM = 64
QK_NOPE_DIM = 64
QK_ROPE_DIM = 32
QK_HEAD_DIM = QK_NOPE_DIM + QK_ROPE_DIM
N_GROUPS = 4
EXPERTS_PER_GROUP = 8
N_EXPERTS = N_GROUPS * EXPERTS_PER_GROUP
TOP_K = 2
ROPE_THETA = 10000.0
EPS = 1e-6
LOG2_E = 1.4426950408889634

LANES = 128
SUBLANES = 8
CONV_HALO = 32

TM_PROJ = 512
TS_CONV = 256
CONV_ROWS = 32
TQ_ATTN = 512
DENOM_ROWS = 16
HEADS_PER_STEP = 4
T_MOE = 256
IDX_SLOT = 1024
TB_DISPATCH = IDX_SLOT // TOP_K
TM_COMB = IDX_SLOT // TOP_K
VMEM_LIMIT = 48 * 1024 * 1024


def _sigmoid(v):
    return 1.0 / (1.0 + jnp.exp(-v))


def _cparams(sem):
    return pltpu.CompilerParams(dimension_semantics=sem, vmem_limit_bytes=VMEM_LIMIT)


def _ada_kernel(c_ref, w_ref, b_ref, o_ref):
    c = c_ref[...]
    s = c * _sigmoid(c)
    o_ref[...] = jnp.dot(s, w_ref[...], preferred_element_type=F32,
                         precision=lax.Precision.HIGHEST) + b_ref[...]


def _ada(c, w_ada, b_ada):
    b, d = c.shape
    n_out = w_ada.shape[1]
    return pl.pallas_call(
        _ada_kernel,
        grid=(n_out // d,),
        in_specs=[pl.BlockSpec((b, d), lambda j: (0, 0)),
                  pl.BlockSpec((d, d), lambda j: (0, j)),
                  pl.BlockSpec((1, d), lambda j: (0, j))],
        out_specs=pl.BlockSpec((b, d), lambda j: (0, j)),
        out_shape=jax.ShapeDtypeStruct((b, n_out), F32),
        compiler_params=_cparams(("arbitrary",)),
        name="ada",
    )(c, w_ada, b_ada.reshape(1, n_out))


def _head_norm_rope_t(tt, gain, cos_t, sin_t):
    live = tt[:QK_HEAD_DIM]
    r = lax.rsqrt(jnp.sum(live * live, axis=0, keepdims=True) * (1.0 / QK_HEAD_DIM) + EPS)
    tn = live * r * gain[:QK_HEAD_DIM]
    half = QK_ROPE_DIM // 2
    lo = tn[QK_NOPE_DIM:QK_NOPE_DIM + half]
    hi = tn[QK_NOPE_DIM + half:QK_HEAD_DIM]
    return jnp.concatenate([tn[:QK_NOPE_DIM], lo * cos_t - hi * sin_t, hi * cos_t + lo * sin_t,
                            jnp.zeros((LANES - QK_HEAD_DIM, tt.shape[1]), F32)], axis=0)


def _inproj_kernel(x_ref, mod_ref, nmix_ref, win_ref, qan_ref, wqt_ref, kvan_ref, wkt_ref, wvt_ref,
                   qn_ref, kn_ref, cos_ref, sin_ref, u_ref, qt_ref, k_ref, vt_ref):
    x = x_ref[...]
    sh_a = mod_ref[0, 0:1, :]
    sc_a = mod_ref[0, 1:2, :]
    y = x * lax.rsqrt(jnp.mean(x * x, axis=-1, keepdims=True) + EPS) * nmix_ref[...]
    h = y * (1.0 + sc_a) + sh_a
    proj = jnp.dot(h.astype(BF16), win_ref[...], preferred_element_type=F32)
    c_conv = u_ref.shape[-1]
    val = proj[:, :c_conv]
    gate = proj[:, c_conv:2 * c_conv]
    u_ref[...] = (val * _sigmoid(gate)).astype(BF16)
    o = 2 * c_conv
    q_rank = wqt_ref.shape[1]
    kv_rank = wkt_ref.shape[1]
    cq = proj[:, o:o + q_rank]
    ckv = proj[:, o + q_rank:o + q_rank + kv_rank]
    kr = proj[:, o + q_rank + kv_rank:o + q_rank + kv_rank + LANES]
    cqn = cq * lax.rsqrt(jnp.mean(cq * cq, axis=-1, keepdims=True) + EPS) * qan_ref[...]
    ckvn = ckv * lax.rsqrt(jnp.mean(ckv * ckv, axis=-1, keepdims=True) + EPS) * kvan_ref[...]
    cqn_t = cqn.T.astype(BF16)
    ckvn_t = ckvn.T.astype(BF16)
    kr_t = kr.T
    qt_all = jnp.dot(wqt_ref[...], cqn_t, preferred_element_type=F32)
    kt_all = jnp.dot(wkt_ref[...], ckvn_t, preferred_element_type=F32)
    vt_ref[0] = jnp.dot(wvt_ref[...], ckvn_t, preferred_element_type=F32).astype(BF16)
    cos_t = cos_ref[...]
    sin_t = sin_ref[...]
    scale = QK_HEAD_DIM ** -0.5 * LOG2_E
    for hd in range(N_HEADS):
        qh = qt_all[hd * LANES:(hd + 1) * LANES]
        qt_ref[0, hd] = (_head_norm_rope_t(qh, qn_ref[...], cos_t, sin_t) * scale).astype(BF16)
        kh = kt_all[hd * LANES:(hd + 1) * LANES] + kr_t
        k_ref[0, hd] = _head_norm_rope_t(kh, kn_ref[...], cos_t, sin_t).T.astype(BF16)


def _inproj(x2, mod3, nmix, win_p, qan, wqt_p, kvan, wkt_p, wvt, qn_t, kn_t, cos_t, sin_t, b, s):
    n, d = x2.shape
    tm = TM_PROJ
    tpb = s // tm
    c_conv = (win_p.shape[1] - wqt_p.shape[1] - wkt_p.shape[1] - LANES) // 2
    half = QK_ROPE_DIM // 2
    full = lambda a: pl.BlockSpec(a.shape, lambda i: (0,) * a.ndim)
    return pl.pallas_call(
        _inproj_kernel,
        grid=(n // tm,),
        in_specs=[pl.BlockSpec((tm, d), lambda i: (i, 0)),
                  pl.BlockSpec((1, 6, d), lambda i: (i // tpb, 0, 0)),
                  full(nmix), full(win_p), full(qan), full(wqt_p), full(kvan), full(wkt_p), full(wvt),
                  full(qn_t), full(kn_t),
                  pl.BlockSpec((half, tm), lambda i: (0, i)),
                  pl.BlockSpec((half, tm), lambda i: (0, i))],
        out_specs=[pl.BlockSpec((tm, c_conv), lambda i: (i, 0)),
                   pl.BlockSpec((1, N_HEADS, LANES, tm), lambda i: (i // tpb, 0, 0, i % tpb)),
                   pl.BlockSpec((1, N_HEADS, tm, LANES), lambda i: (i // tpb, 0, i % tpb, 0)),
                   pl.BlockSpec((1, c_conv, tm), lambda i: (i // tpb, 0, i % tpb))],
        out_shape=[jax.ShapeDtypeStruct((n, c_conv), BF16),
                   jax.ShapeDtypeStruct((b, N_HEADS, LANES, s), BF16),
                   jax.ShapeDtypeStruct((b, N_HEADS, s, LANES), BF16),
                   jax.ShapeDtypeStruct((b, N_HEADS * V_HEAD_DIM, s), BF16)],
        compiler_params=_cparams(("arbitrary",)),
        name="inproj",
    )(x2, mod3, nmix, win_p, qan, wqt_p, kvan, wkt_p, wvt, qn_t, kn_t, cos_t, sin_t)


def _conv_kernel(prev_ref, cur_ref, w_ref, b_ref, g_ref, beta_ref, o_ref, buf_ref, shift_ref):
    j = pl.program_id(1)
    ts = cur_ref.shape[1]

    @pl.when(j == 0)
    def _():
        buf_ref[0:CONV_HALO, :] = jnp.zeros((CONV_HALO, buf_ref.shape[1]), F32)

    @pl.when(j > 0)
    def _():
        buf_ref[0:CONV_HALO, :] = prev_ref[0].astype(F32)

    buf_ref[CONV_HALO:CONV_HALO + ts, :] = cur_ref[0].astype(F32)
    span = CONV_HALO + ts - SUBLANES
    for sh in range(1, SUBLANES):
        shift_ref[sh - 1, 0:span, :] = buf_ref[sh:sh + span, :]
    first = CONV_HALO - (CONV_K - 1)
    for r0 in range(0, ts, CONV_ROWS):
        acc = None
        for kk in range(CONV_K):
            sh = (first + kk) % SUBLANES
            base = r0 + (first + kk) - sh
            if sh == 0:
                win = buf_ref[base:base + CONV_ROWS, :]
            else:
                win = shift_ref[sh - 1, base:base + CONV_ROWS, :]
            term = win * w_ref[kk:kk + 1, :]
            acc = term if acc is None else acc + term
        dw = acc + b_ref[...]
        mu = jnp.mean(dw, axis=-1, keepdims=True)
        cen = dw - mu
        var = jnp.mean(cen * cen, axis=-1, keepdims=True)
        z = cen * lax.rsqrt(var + EPS) * g_ref[...] + beta_ref[...]
        o_ref[0, r0:r0 + CONV_ROWS, :] = (z * _sigmoid(z)).astype(BF16)


def _conv(u3, conv_w_p, conv_b, ln_g, ln_b):
    b, s, c = u3.shape
    ts = TS_CONV
    hb = ts // CONV_HALO
    return pl.pallas_call(
        _conv_kernel,
        grid=(b, s // ts),
        in_specs=[pl.BlockSpec((1, CONV_HALO, c), lambda bi, j: (bi, jnp.maximum(j * hb - 1, 0), 0)),
                  pl.BlockSpec((1, ts, c), lambda bi, j: (bi, j, 0)),
                  pl.BlockSpec(conv_w_p.shape, lambda bi, j: (0, 0)),
                  pl.BlockSpec((1, c), lambda bi, j: (0, 0)),
                  pl.BlockSpec((1, c), lambda bi, j: (0, 0)),
                  pl.BlockSpec((1, c), lambda bi, j: (0, 0))],
        out_specs=pl.BlockSpec((1, ts, c), lambda bi, j: (bi, j, 0)),
        out_shape=jax.ShapeDtypeStruct((b, s, c), BF16),
        scratch_shapes=[pltpu.VMEM((CONV_HALO + ts, c), F32),
                        pltpu.VMEM((SUBLANES - 1, CONV_HALO + ts - SUBLANES, c), F32)],
        compiler_params=_cparams(("arbitrary", "arbitrary")),
        name="conv",
    )(u3, u3, conv_w_p, conv_b, ln_g, ln_b)


def _attn_kernel(qt_ref, k_ref, vt_ref, o_ref, sa_ref, sb_ref):
    qi = pl.program_id(2)
    tq = qt_ref.shape[3]
    heads = qt_ref.shape[1]
    key_chunk = lax.broadcasted_iota(jnp.int32, (tq, tq), 0) // CHUNK
    qry_chunk = lax.broadcasted_iota(jnp.int32, (tq, tq), 1) // CHUNK
    diag_mask = key_chunk <= qry_chunk
    ones_rows = (lax.broadcasted_iota(jnp.int32, (DENOM_ROWS, tq), 0) == 0).astype(BF16)

    def scores(jt, s_ref):
        start = pl.multiple_of(jt * tq, tq)
        for hh in range(heads):
            kj = k_ref[0, hh, pl.ds(start, tq), :]
            s_ref[hh] = jnp.dot(kj, qt_ref[0, hh], preferred_element_type=F32)

    def consume(jt, s_ref, carry, masked):
        start = pl.multiple_of(jt * tq, tq)
        new = []
        for hh in range(heads):
            m, acc = carry[hh]
            st = s_ref[hh]
            if masked:
                st = jnp.where(diag_mask, st, -jnp.inf)
            vj = vt_ref[0, hh * V_HEAD_DIM:(hh + 1) * V_HEAD_DIM, pl.ds(start, tq)]
            m_new = jnp.maximum(m, jnp.max(st, axis=0, keepdims=True))
            alpha = jnp.exp2(m - m_new)
            p = jnp.exp2(st - m_new)
            v_ext = jnp.concatenate([vj, ones_rows], axis=0)
            acc_new = alpha * acc + jnp.dot(v_ext, p.astype(BF16), preferred_element_type=F32)
            new.append((m_new, acc_new))
        return tuple(new)

    def body(u, carry):
        scores(2 * u + 1, sb_ref)
        carry = consume(2 * u, sa_ref, carry, False)
        scores(2 * u + 2, sa_ref)
        carry = consume(2 * u + 1, sb_ref, carry, False)
        return carry

    init = tuple((jnp.full((1, tq), -jnp.inf, F32), jnp.zeros((V_HEAD_DIM + DENOM_ROWS, tq), F32))
                 for _ in range(heads))
    scores(0, sa_ref)
    carry = lax.fori_loop(0, qi // 2, body, init)

    def last_even(carry):
        return consume(qi, sa_ref, carry, True)

    def last_odd(carry):
        scores(qi, sb_ref)
        carry = consume(qi - 1, sa_ref, carry, False)
        return consume(qi, sb_ref, carry, True)

    carry = lax.cond(qi % 2 == 1, last_odd, last_even, carry)
    out_t = jnp.concatenate([acc[:V_HEAD_DIM] / acc[V_HEAD_DIM:V_HEAD_DIM + 1] for (_, acc) in carry],
                            axis=0)
    o_ref[0] = out_t.T.astype(BF16)


def _attn(qt4, k4, vt3):
    b, nh, s, _ = k4.shape
    tq = TQ_ATTN
    hps = HEADS_PER_STEP
    return pl.pallas_call(
        _attn_kernel,
        scratch_shapes=[pltpu.VMEM((hps, tq, tq), F32), pltpu.VMEM((hps, tq, tq), F32)],
        grid=(b, nh // hps, s // tq),
        in_specs=[pl.BlockSpec((1, hps, LANES, tq), lambda bi, hp, qi: (bi, hp, 0, qi)),
                  pl.BlockSpec((1, hps, s, LANES), lambda bi, hp, qi: (bi, hp, 0, 0)),
                  pl.BlockSpec((1, hps * V_HEAD_DIM, s), lambda bi, hp, qi: (bi, hp, 0))],
        out_specs=pl.BlockSpec((1, tq, hps * V_HEAD_DIM), lambda bi, hp, qi: (bi, qi, hp)),
        out_shape=jax.ShapeDtypeStruct((b, s, nh * V_HEAD_DIM), BF16),
        compiler_params=_cparams(("arbitrary", "arbitrary", "arbitrary")),
        name="attn",
    )(qt4, k4, vt3)


def _ffn_input(x1, nffn, mod_ref):
    sh_f = mod_ref[0, 3:4, :]
    sc_f = mod_ref[0, 4:5, :]
    y = x1 * lax.rsqrt(jnp.mean(x1 * x1, axis=-1, keepdims=True) + EPS) * nffn
    return y * (1.0 + sc_f) + sh_f


def _outproj_kernel(yc_ref, ya_ref, x_ref, mod_ref, wout_ref, nffn_ref, wr_ref, br_ref,
                    x1_ref, route_ref):
    g_a = mod_ref[0, 2:3, :]
    ycat = jnp.concatenate([yc_ref[...], ya_ref[...]], axis=-1)
    mixed = jnp.dot(ycat, wout_ref[...], preferred_element_type=F32)
    x1 = x_ref[...] + g_a * mixed
    x1_ref[...] = x1
    h2 = _ffn_input(x1, nffn_ref[...], mod_ref)
    logits = jnp.dot(h2.astype(BF16), wr_ref[...], preferred_element_type=F32) + br_ref[...]
    lane = lax.broadcasted_iota(jnp.int32, logits.shape, 1)
    lane_f = lane.astype(F32)
    big = float(LANES)
    is_g = (lane >= N_EXPERTS) & (lane < N_EXPERTS + N_GROUPS)
    gl = jnp.where(is_g, logits, -jnp.inf)
    gmax = jnp.max(gl, axis=-1, keepdims=True)
    g_top = 1.0 / jnp.sum(jnp.exp(gl - gmax), axis=-1, keepdims=True)
    g_idx = jnp.min(jnp.where(gl == gmax, lane_f - N_EXPERTS, big), axis=-1, keepdims=True)
    in_grp = (lane < N_EXPERTS) & ((lane // EXPERTS_PER_GROUP).astype(F32) == g_idx)
    el = jnp.where(in_grp, logits, -jnp.inf)
    e1 = jnp.max(el, axis=-1, keepdims=True)
    i1 = jnp.min(jnp.where(el == e1, lane_f, big), axis=-1, keepdims=True)
    el2 = jnp.where(lane_f == i1, -jnp.inf, el)
    e2 = jnp.max(el2, axis=-1, keepdims=True)
    i2 = jnp.min(jnp.where(el2 == e2, lane_f, big), axis=-1, keepdims=True)
    t = jnp.exp(e2 - e1)
    w1 = g_top / (1.0 + t)
    w2 = g_top * t / (1.0 + t)
    route_ref[...] = jnp.where(lane == 0, w1, jnp.where(lane == 1, w2, jnp.where(
        lane == 2, i1, jnp.where(lane == 3, i2, 0.0))))


def _outproj(yc, ya, x2, mod3, wout, nffn, wr, br, s):
    n, d = x2.shape
    tm = TM_PROJ
    tpb = s // tm
    c = yc.shape[1]
    full = lambda a: pl.BlockSpec(a.shape, lambda i: (0,) * a.ndim)
    return pl.pallas_call(
        _outproj_kernel,
        grid=(n // tm,),
        in_specs=[pl.BlockSpec((tm, c), lambda i: (i, 0)),
                  pl.BlockSpec((tm, c), lambda i: (i, 0)),
                  pl.BlockSpec((tm, d), lambda i: (i, 0)),
                  pl.BlockSpec((1, 6, d), lambda i: (i // tpb, 0, 0)),
                  full(wout), full(nffn), full(wr), full(br)],
        out_specs=[pl.BlockSpec((tm, d), lambda i: (i, 0)),
                   pl.BlockSpec((tm, LANES), lambda i: (i, 0))],
        out_shape=[jax.ShapeDtypeStruct((n, d), F32),
                   jax.ShapeDtypeStruct((n, LANES), F32)],
        compiler_params=_cparams(("arbitrary",)),
        name="outproj",
    )(yc, ya, x2, mod3, wout, nffn, wr, br)


def _store_token_tiles(ref, row0, val, lead=()):
    rows = val.shape[0]
    for cc in range(SUBLANES):
        ref[lead + (pl.ds(row0 * SUBLANES + cc, rows, stride=SUBLANES), slice(None))] = (
            val[:, cc * LANES:(cc + 1) * LANES])


def _load_token_tiles(ref, row0, rows, lead=()):
    parts = [ref[lead + (pl.ds(row0 * SUBLANES + cc, rows, stride=SUBLANES), slice(None))]
             for cc in range(SUBLANES)]
    return jnp.concatenate(parts, axis=-1)


def _idx_slot(ismem, s):
    return ismem.at[pl.ds(pl.multiple_of(s * IDX_SLOT, IDX_SLOT), IDX_SLOT)]


def _gather_rows_step(i, n_steps, idx_hbm, src_hbm, ismem, buf, isem, rsem):
    n_rows = IDX_SLOT
    slot = i % 2
    nslot = 1 - slot

    def idx_copy(step, s):
        return pltpu.make_async_copy(idx_hbm.at[step], _idx_slot(ismem, s), isem.at[s])

    def row_copy(row8, r, s):
        return pltpu.make_async_copy(src_hbm.at[pl.ds(pl.multiple_of(row8, SUBLANES), SUBLANES)],
                                     buf.at[s, pl.ds(pl.multiple_of(r * SUBLANES, SUBLANES), SUBLANES)],
                                     rsem.at[s])

    def issue_rows(s):
        def body(r, carry):
            row_copy(ismem[s * IDX_SLOT + r], r, s).start()
            return carry
        lax.fori_loop(0, n_rows, body, 0, unroll=8)

    def wait_rows(s):
        def body(r, carry):
            row_copy(0, r, s).wait()
            return carry
        lax.fori_loop(0, n_rows, body, 0, unroll=8)

    @pl.when(i == 0)
    def _():
        idx_copy(0, 0).start()
        idx_copy(0, 0).wait()
        issue_rows(0)

        @pl.when(n_steps > 1)
        def _():
            idx_copy(1, 1).start()

    @pl.when(i + 1 < n_steps)
    def _():
        idx_copy(i + 1, nslot).wait()
        issue_rows(nslot)

    @pl.when(i + 2 < n_steps)
    def _():
        idx_copy(i + 2, slot).start()

    wait_rows(slot)


def _dispatch_kernel(pad_row_ref, pad_cnt_ref, nv_ref, pos_hbm, x1_ref, mod_ref, nffn_ref, xs_hbm,
                     ismem, hbuf, zero_ref, isem, csem, zsem):
    i = pl.program_id(0)
    n_steps = pl.num_programs(0)
    n_pairs = IDX_SLOT
    tb = n_pairs // TOP_K
    slot = i % 2

    def idx_copy(step, s):
        return pltpu.make_async_copy(pos_hbm.at[step], _idx_slot(ismem, s), isem.at[s])

    def tile_copy(r, row8, s):
        return pltpu.make_async_copy(
            hbuf.at[s, pl.ds(pl.multiple_of(r * SUBLANES, SUBLANES), SUBLANES)],
            xs_hbm.at[pl.ds(pl.multiple_of(row8, SUBLANES), SUBLANES)], csem.at[s])

    def zero_copy(row8):
        return pltpu.make_async_copy(
            zero_ref.at[pl.ds(0, SUBLANES)],
            xs_hbm.at[pl.ds(pl.multiple_of(row8, SUBLANES), SUBLANES)], zsem.at[0])

    def for_pad_tiles(fn):
        def per_expert(e, carry):
            def per_tile(j, c2):
                fn(pad_row_ref[e] + j * SUBLANES)
                return c2
            return lax.fori_loop(0, pad_cnt_ref[e], per_tile, carry)
        lax.fori_loop(0, N_EXPERTS, per_expert, 0)

    tile_rows8 = zero_ref.shape[0]
    n_tiles = xs_hbm.shape[0] // tile_rows8

    def tail_copy(j):
        return pltpu.make_async_copy(
            zero_ref, xs_hbm.at[pl.ds(pl.multiple_of(j * tile_rows8, tile_rows8), tile_rows8)], zsem.at[1])

    def for_tail_tiles(fn):
        def body(j, carry):
            fn(j)
            return carry
        lax.fori_loop(nv_ref[0], n_tiles, body, 0)

    @pl.when(i == 0)
    def _():
        idx_copy(0, 0).start()
        zero_ref[...] = jnp.zeros(zero_ref.shape, F32)
        for_pad_tiles(lambda row8: zero_copy(row8).start())
        for_tail_tiles(lambda j: tail_copy(j).start())

    idx_copy(i, slot).wait()

    @pl.when(i + 1 < n_steps)
    def _():
        idx_copy(i + 1, 1 - slot).start()

    _store_token_tiles(hbuf, 0, _ffn_input(x1_ref[...], nffn_ref[...], mod_ref), (slot,))

    def issue(r, carry):
        for kk in range(TOP_K):
            tile_copy(r, ismem[slot * IDX_SLOT + kk * tb + r], slot).start()
        return carry
    lax.fori_loop(0, tb, issue, 0, unroll=4)

    def drain(s):
        def body(r, carry):
            tile_copy(0, 0, s).wait()
            return carry
        lax.fori_loop(0, n_pairs, body, 0, unroll=8)

    @pl.when(i > 0)
    def _():
        drain(1 - slot)

    @pl.when(i == n_steps - 1)
    def _():
        drain(slot)
        for_pad_tiles(lambda row8: zero_copy(0).wait())
        for_tail_tiles(lambda j: tail_copy(0).wait())


def _dispatch(pad_row8, pad_cnt, n_valid, pos8_blocks, x1, mod3, nffn, n_rows, s):
    n_blocks, n_pairs = pos8_blocks.shape
    n, d = x1.shape
    tb = n_pairs // TOP_K
    tpb = s // tb
    grid_spec = pltpu.PrefetchScalarGridSpec(
        num_scalar_prefetch=3,
        grid=(n_blocks,),
        in_specs=[pl.BlockSpec(memory_space=pl.ANY),
                  pl.BlockSpec((tb, d), lambda i, pr, pc, nv: (i, 0)),
                  pl.BlockSpec((1, 6, d), lambda i, pr, pc, nv: (i // tpb, 0, 0)),
                  pl.BlockSpec((1, d), lambda i, pr, pc, nv: (0, 0))],
        out_specs=pl.BlockSpec(memory_space=pl.ANY),
        scratch_shapes=[pltpu.SMEM((2 * IDX_SLOT,), jnp.int32),
                        pltpu.VMEM((2, tb * SUBLANES, LANES), F32),
                        pltpu.VMEM((T_MOE * SUBLANES, LANES), F32),
                        pltpu.SemaphoreType.DMA((2,)),
                        pltpu.SemaphoreType.DMA((2,)),
                        pltpu.SemaphoreType.DMA((2,))])
    return pl.pallas_call(
        _dispatch_kernel,
        grid_spec=grid_spec,
        out_shape=jax.ShapeDtypeStruct((n_rows * SUBLANES, LANES), F32),
        compiler_params=_cparams(("arbitrary",)),
        name="dispatch",
    )(pad_row8, pad_cnt, n_valid, pos8_blocks, x1, mod3, nffn)


def _moe_kernel(te_ref, nv_ref, xs_ref, wg_ref, wu_ref, wd_ref, y_ref):
    i = pl.program_id(0)
    t = xs_ref.shape[0] // SUBLANES

    @pl.when(i < nv_ref[0])
    def _():
        xg = _load_token_tiles(xs_ref, 0, t).astype(BF16)
        g = jnp.dot(xg, wg_ref[0], preferred_element_type=F32)
        u = jnp.dot(xg, wu_ref[0], preferred_element_type=F32)
        a = (g * _sigmoid(g) * u).astype(BF16)
        _store_token_tiles(y_ref, 0, jnp.dot(a, wd_ref[0], preferred_element_type=F32))

    @pl.when(i >= nv_ref[0])
    def _():
        y_ref[...] = jnp.zeros(y_ref.shape, F32)


def _moe(tile_expert, n_valid, xs, wg, wu, wd):
    t = T_MOE
    n_tiles = xs.shape[0] // (t * SUBLANES)
    d = wg.shape[1]
    de = wg.shape[2]
    grid_spec = pltpu.PrefetchScalarGridSpec(
        num_scalar_prefetch=2,
        grid=(n_tiles,),
        in_specs=[pl.BlockSpec((t * SUBLANES, LANES), lambda i, te, nv: (jnp.minimum(i, nv[0] - 1), 0)),
                  pl.BlockSpec((1, d, de), lambda i, te, nv: (te[i], 0, 0)),
                  pl.BlockSpec((1, d, de), lambda i, te, nv: (te[i], 0, 0)),
                  pl.BlockSpec((1, de, d), lambda i, te, nv: (te[i], 0, 0))],
        out_specs=pl.BlockSpec((t * SUBLANES, LANES), lambda i, te, nv: (i, 0)))
    return pl.pallas_call(
        _moe_kernel,
        grid_spec=grid_spec,
        out_shape=jax.ShapeDtypeStruct(xs.shape, F32),
        compiler_params=_cparams(("arbitrary",)),
        name="moe",
    )(tile_expert, n_valid, xs, wg, wu, wd)


def _combine_kernel(pos_hbm, y_hbm, x1_ref, route_ref, mod_ref, o_ref, ismem, ybuf, isem, rsem):
    i = pl.program_id(0)
    tm = x1_ref.shape[0]
    _gather_rows_step(i, pl.num_programs(0), pos_hbm, y_hbm, ismem, ybuf, isem, rsem)
    slot = i % 2
    g_f = mod_ref[0, 5:6, :]
    w0 = route_ref[:, 0:1]
    w1 = route_ref[:, 1:2]
    y = (w0 * _load_token_tiles(ybuf, 0, tm, (slot,)) + w1 * _load_token_tiles(ybuf, tm, tm, (slot,)))
    o_ref[...] = x1_ref[...] + g_f * y


def _combine(pos2, y_sorted, x1, route, mod3, s):
    n, d = x1.shape
    tm = TM_COMB
    tpb = s // tm
    return pl.pallas_call(
        _combine_kernel,
        grid=(n // tm,),
        in_specs=[pl.BlockSpec(memory_space=pl.ANY),
                  pl.BlockSpec(memory_space=pl.ANY),
                  pl.BlockSpec((tm, d), lambda i: (i, 0)),
                  pl.BlockSpec((tm, LANES), lambda i: (i, 0)),
                  pl.BlockSpec((1, 6, d), lambda i: (i // tpb, 0, 0))],
        out_specs=pl.BlockSpec((tm, d), lambda i: (i, 0)),
        out_shape=jax.ShapeDtypeStruct((n, d), F32),
        scratch_shapes=[pltpu.SMEM((2 * IDX_SLOT,), jnp.int32),
                        pltpu.VMEM((2, TOP_K * tm * SUBLANES, LANES), F32),
                        pltpu.SemaphoreType.DMA((2,)),
                        pltpu.SemaphoreType.DMA((2,))],
        compiler_params=_cparams(("arbitrary",)),
        name="combine",
    )(pos2, y_sorted, x1, route, mod3)


def _routing_tables(route, n):
    t = T_MOE
    flat_e = route[:, 2:4].astype(jnp.int32).reshape(-1)
    onehot = (flat_e[:, None] == jnp.arange(N_EXPERTS, dtype=jnp.int32)[None, :]).astype(jnp.int32)
    csum = jnp.cumsum(onehot, axis=0)
    rank = jnp.sum(onehot * csum, axis=1) - 1
    counts = csum[-1]
    ptiles = (counts + t - 1) // t
    tile_end = jnp.cumsum(ptiles)
    tile_start = tile_end - ptiles
    pos = jnp.sum(onehot * (tile_start * t)[None, :], axis=1) + rank
    n_tiles = (TOP_K * n) // t + N_EXPERTS
    n_valid = tile_end[-1]
    tile_ids = jnp.arange(n_tiles, dtype=jnp.int32)
    te = jnp.sum((tile_ids[:, None] >= tile_end[None, :]).astype(jnp.int32), axis=1)
    te_last = jnp.sum((n_valid - 1 >= tile_end).astype(jnp.int32))
    tile_expert = jnp.minimum(jnp.where(tile_ids < n_valid, te, te_last), N_EXPERTS - 1)
    pos8 = (pos * SUBLANES).astype(jnp.int32).reshape(n, TOP_K)
    blocks = lambda tb: pos8.reshape(n // tb, tb, TOP_K).transpose(0, 2, 1).reshape(n // tb, TOP_K * tb)
    pad_row8 = ((tile_start * t + counts) * SUBLANES).astype(jnp.int32)
    pad_cnt = (ptiles * t - counts).astype(jnp.int32)
    return (tile_expert.astype(jnp.int32), n_valid.astype(jnp.int32).reshape(1), pad_row8, pad_cnt,
            blocks(TB_DISPATCH), blocks(TM_COMB), n_tiles * t)


def _pad_lanes(a, left, total):
    pad = [(0, 0)] * (a.ndim - 1) + [(left, total - left - a.shape[-1])]
    return jnp.pad(a, pad)


def _layer(x2, c, cos_t, sin_t, b, s, w_ada, b_ada, norm_mix, w_in, conv_w, conv_b, conv_ln_g, conv_ln_b,
           q_a_norm, w_q_b, kv_a_norm, w_kv_b, q_norm, k_norm, w_out, norm_ffn, w_group, b_group,
           w_expert, b_expert, w_gate_e, w_up_e, w_down_e):
    n, d = x2.shape
    c_conv = conv_w.shape[1]
    q_rank = q_a_norm.shape[0]
    kv_rank = kv_a_norm.shape[0]
    row = lambda a: a.reshape(1, -1)

    o1 = 2 * c_conv
    o2 = o1 + q_rank
    o3 = o2 + kv_rank
    win_p = jnp.concatenate([w_in[:, :o3], _pad_lanes(w_in[:, o3:], QK_NOPE_DIM, LANES)], axis=1).astype(BF16)
    wq_p = _pad_lanes(w_q_b.reshape(q_rank, N_HEADS, QK_HEAD_DIM), 0, LANES).reshape(q_rank, N_HEADS * LANES)
    wkv = w_kv_b.reshape(kv_rank, N_HEADS, QK_NOPE_DIM + V_HEAD_DIM)
    wk_p = _pad_lanes(wkv[..., :QK_NOPE_DIM], 0, LANES).reshape(kv_rank, N_HEADS * LANES)
    wv = wkv[..., QK_NOPE_DIM:].reshape(kv_rank, N_HEADS * V_HEAD_DIM)
    gain_t = lambda g: jnp.broadcast_to(_pad_lanes(row(g), 0, LANES).reshape(LANES, 1), (LANES, TM_PROJ))
    qn_t = gain_t(q_norm)
    kn_t = gain_t(k_norm)
    wr = _pad_lanes(jnp.concatenate([w_expert, w_group], axis=1), 0, LANES).astype(BF16)
    br = _pad_lanes(row(jnp.concatenate([b_expert, b_group])), 0, LANES)
    conv_w_p = jnp.pad(conv_w, ((0, CONV_HALO - CONV_K), (0, 0)))

    mod3 = _ada(c, w_ada, b_ada).reshape(b, 6, d)
    u, qt4, k4, vt3 = _inproj(x2, mod3, row(norm_mix), win_p, row(q_a_norm), wq_p.T.astype(BF16),
                              row(kv_a_norm), wk_p.T.astype(BF16), wv.T.astype(BF16), qn_t, kn_t,
                              cos_t, sin_t, b, s)
    y_conv = _conv(u.reshape(b, s, c_conv), conv_w_p, row(conv_b), row(conv_ln_g), row(conv_ln_b))
    y_attn = _attn(qt4, k4, vt3)
    x1, route = _outproj(y_conv.reshape(n, c_conv), y_attn.reshape(n, -1), x2, mod3,
                         w_out.astype(BF16), row(norm_ffn), wr, br, s)
    tile_expert, n_valid, pad_row8, pad_cnt, pos8_disp, pos8_comb, n_rows = _routing_tables(route, n)
    xs = _dispatch(pad_row8, pad_cnt, n_valid, pos8_disp, x1, mod3, row(norm_ffn), n_rows, s)
    ys = _moe(tile_expert, n_valid, xs, w_gate_e.astype(BF16), w_up_e.astype(BF16), w_down_e.astype(BF16))
    return _combine(pos8_comb, ys, x1, route, mod3, s)


def kernel(x, c, positions, w_ada, b_ada, norm_mix, w_in, conv_w, conv_b, conv_ln_g, conv_ln_b, q_a_norm,
           w_q_b, kv_a_norm, w_kv_b, q_norm, k_norm, w_out, norm_ffn, w_group, b_group, w_expert, b_expert,
           w_gate_e, w_up_e, w_down_e):
    b, s, d = x.shape
    n = b * s
    inv_freq = ROPE_THETA ** (-jnp.arange(0, QK_ROPE_DIM, 2, dtype=F32) / QK_ROPE_DIM)
    ang = inv_freq[:, None] * positions.astype(F32).reshape(1, n)
    cos_t, sin_t = jnp.cos(ang), jnp.sin(ang)
    x2 = x.reshape(n, d)
    for l in range(w_ada.shape[0]):
        x2 = _layer(x2, c, cos_t, sin_t, b, s, w_ada[l], b_ada[l], norm_mix[l], w_in[l], conv_w[l], conv_b[l],
                    conv_ln_g[l], conv_ln_b[l], q_a_norm[l], w_q_b[l], kv_a_norm[l], w_kv_b[l], q_norm[l],
                    k_norm[l], w_out[l], norm_ffn[l], w_group[l], b_group[l], w_expert[l], b_expert[l],
                    w_gate_e[l], w_up_e[l], w_down_e[l])
    return x2.reshape(b, s, d)
```

```python
import functools

import jax
import jax.numpy as jnp
from jax import lax
from jax.experimental import pallas as pl
from jax.experimental.pallas import tpu as pltpu

F32 = jnp.float32
BF16 = jnp.bfloat16

CHUNK = 64
CONV_K = 31
N_HEADS = 8
V_HEAD_DIM = 64
QK_NOPE_DIM = 64
QK_ROPE_DIM = 32
QK_HEAD_DIM = QK_NOPE_DIM + QK_ROPE_DIM
N_GROUPS = 4
EXPERTS_PER_GROUP = 8
N_EXPERTS = N_GROUPS * EXPERTS_PER_GROUP
TOP_K = 2
ROPE_THETA = 10000.0
EPS = 1e-6
LOG2_E = 1.4426950408889634

LANES = 128
SUBLANES = 8
TOKEN_ROWS = 8
TOKEN_PITCH = TOKEN_ROWS + 1
CONV_HALO = 32

TM_PROJ = 512
TS_CONV = 256
CONV_ROWS = 32
TQ_ATTN = 512
DENOM_ROWS = 16
HEADS_PER_STEP = 4
T_MOE = 256
IDX_SLOT = 1024
TB_DISPATCH = IDX_SLOT // TOP_K
TM_COMB = IDX_SLOT // TOP_K
VMEM_LIMIT = 48 * 1024 * 1024


def _sigmoid(v):
    return 1.0 / (1.0 + jnp.exp(-v))


def _cparams(sem):
    return pltpu.CompilerParams(dimension_semantics=sem, vmem_limit_bytes=VMEM_LIMIT)


def _ada_kernel(c_ref, w_ref, b_ref, o_ref):
    c = c_ref[...]
    s = c * _sigmoid(c)
    o_ref[...] = jnp.dot(s, w_ref[...], preferred_element_type=F32,
                         precision=lax.Precision.HIGHEST) + b_ref[...]


def _ada(c, w_ada, b_ada):
    b, d = c.shape
    n_out = w_ada.shape[1]
    return pl.pallas_call(
        _ada_kernel,
        grid=(n_out // d,),
        in_specs=[pl.BlockSpec((b, d), lambda j: (0, 0)),
                  pl.BlockSpec((d, d), lambda j: (0, j)),
                  pl.BlockSpec((1, d), lambda j: (0, j))],
        out_specs=pl.BlockSpec((b, d), lambda j: (0, j)),
        out_shape=jax.ShapeDtypeStruct((b, n_out), F32),
        compiler_params=_cparams(("arbitrary",)),
        name="ada",
    )(c, w_ada, b_ada.reshape(1, n_out))


def _head_norm_rope_t(tt, gain, cos_t, sin_t):
    live = tt[:QK_HEAD_DIM]
    r = lax.rsqrt(jnp.sum(live * live, axis=0, keepdims=True) * (1.0 / QK_HEAD_DIM) + EPS)
    tn = live * r * gain[:QK_HEAD_DIM]
    half = QK_ROPE_DIM // 2
    lo = tn[QK_NOPE_DIM:QK_NOPE_DIM + half]
    hi = tn[QK_NOPE_DIM + half:QK_HEAD_DIM]
    return jnp.concatenate([tn[:QK_NOPE_DIM], lo * cos_t - hi * sin_t, hi * cos_t + lo * sin_t,
                            jnp.zeros((LANES - QK_HEAD_DIM, tt.shape[1]), F32)], axis=0)


def _inproj_kernel(x_ref, mod_ref, nmix_ref, win_ref, qan_ref, wqt_ref, kvan_ref, wkt_ref, wvt_ref,
                   qn_ref, kn_ref, cos_ref, sin_ref, u_ref, qt_ref, k_ref, vt_ref):
    x = x_ref[...]
    sh_a = mod_ref[0, 0:1, :]
    sc_a = mod_ref[0, 1:2, :]
    y = x * lax.rsqrt(jnp.mean(x * x, axis=-1, keepdims=True) + EPS) * nmix_ref[...]
    h = y * (1.0 + sc_a) + sh_a
    proj = jnp.dot(h.astype(BF16), win_ref[...], preferred_element_type=F32)
    c_conv = u_ref.shape[-1]
    val = proj[:, :c_conv]
    gate = proj[:, c_conv:2 * c_conv]
    u_ref[...] = (val * _sigmoid(gate)).astype(BF16)
    o = 2 * c_conv
    q_rank = wqt_ref.shape[1]
    kv_rank = wkt_ref.shape[1]
    cq = proj[:, o:o + q_rank]
    ckv = proj[:, o + q_rank:o + q_rank + kv_rank]
    kr = proj[:, o + q_rank + kv_rank:o + q_rank + kv_rank + LANES]
    cqn = cq * lax.rsqrt(jnp.mean(cq * cq, axis=-1, keepdims=True) + EPS) * qan_ref[...]
    ckvn = ckv * lax.rsqrt(jnp.mean(ckv * ckv, axis=-1, keepdims=True) + EPS) * kvan_ref[...]
    cqn_t = cqn.T.astype(BF16)
    ckvn_t = ckvn.T.astype(BF16)
    kr_t = kr.T
    qt_all = jnp.dot(wqt_ref[...], cqn_t, preferred_element_type=F32)
    kt_all = jnp.dot(wkt_ref[...], ckvn_t, preferred_element_type=F32)
    vt_ref[0] = jnp.dot(wvt_ref[...], ckvn_t, preferred_element_type=F32).astype(BF16)
    cos_t = cos_ref[...]
    sin_t = sin_ref[...]
    scale = QK_HEAD_DIM ** -0.5 * LOG2_E
    for hd in range(N_HEADS):
        qh = qt_all[hd * LANES:(hd + 1) * LANES]
        qt_ref[0, hd] = (_head_norm_rope_t(qh, qn_ref[...], cos_t, sin_t) * scale).astype(BF16)
        kh = kt_all[hd * LANES:(hd + 1) * LANES] + kr_t
        k_ref[0, hd] = _head_norm_rope_t(kh, kn_ref[...], cos_t, sin_t).T.astype(BF16)


def _inproj(x2, mod3, nmix, win_p, qan, wqt_p, kvan, wkt_p, wvt, qn_t, kn_t, cos_t, sin_t, b, s):
    n, d = x2.shape
    tm = TM_PROJ
    tpb = s // tm
    c_conv = (win_p.shape[1] - wqt_p.shape[1] - wkt_p.shape[1] - LANES) // 2
    half = QK_ROPE_DIM // 2
    full = lambda a: pl.BlockSpec(a.shape, lambda i: (0,) * a.ndim)
    return pl.pallas_call(
        _inproj_kernel,
        grid=(n // tm,),
        in_specs=[pl.BlockSpec((tm, d), lambda i: (i, 0)),
                  pl.BlockSpec((1, 6, d), lambda i: (i // tpb, 0, 0)),
                  full(nmix), full(win_p), full(qan), full(wqt_p), full(kvan), full(wkt_p), full(wvt),
                  full(qn_t), full(kn_t),
                  pl.BlockSpec((half, tm), lambda i: (0, i)),
                  pl.BlockSpec((half, tm), lambda i: (0, i))],
        out_specs=[pl.BlockSpec((tm, c_conv), lambda i: (i, 0)),
                   pl.BlockSpec((1, N_HEADS, LANES, tm), lambda i: (i // tpb, 0, 0, i % tpb)),
                   pl.BlockSpec((1, N_HEADS, tm, LANES), lambda i: (i // tpb, 0, i % tpb, 0)),
                   pl.BlockSpec((1, c_conv, tm), lambda i: (i // tpb, 0, i % tpb))],
        out_shape=[jax.ShapeDtypeStruct((n, c_conv), BF16),
                   jax.ShapeDtypeStruct((b, N_HEADS, LANES, s), BF16),
                   jax.ShapeDtypeStruct((b, N_HEADS, s, LANES), BF16),
                   jax.ShapeDtypeStruct((b, N_HEADS * V_HEAD_DIM, s), BF16)],
        compiler_params=_cparams(("arbitrary",)),
        name="inproj",
    )(x2, mod3, nmix, win_p, qan, wqt_p, kvan, wkt_p, wvt, qn_t, kn_t, cos_t, sin_t)


def _conv_kernel(prev_ref, cur_ref, w_ref, b_ref, g_ref, beta_ref, o_ref, buf_ref, shift_ref):
    j = pl.program_id(1)
    ts = cur_ref.shape[1]

    @pl.when(j == 0)
    def _():
        buf_ref[0:CONV_HALO, :] = jnp.zeros((CONV_HALO, buf_ref.shape[1]), F32)

    @pl.when(j > 0)
    def _():
        buf_ref[0:CONV_HALO, :] = prev_ref[0].astype(F32)

    buf_ref[CONV_HALO:CONV_HALO + ts, :] = cur_ref[0].astype(F32)
    span = CONV_HALO + ts - SUBLANES
    for sh in range(1, SUBLANES):
        shift_ref[sh - 1, 0:span, :] = buf_ref[sh:sh + span, :]
    first = CONV_HALO - (CONV_K - 1)
    for r0 in range(0, ts, CONV_ROWS):
        acc = None
        for kk in range(CONV_K):
            sh = (first + kk) % SUBLANES
            base = r0 + (first + kk) - sh
            if sh == 0:
                win = buf_ref[base:base + CONV_ROWS, :]
            else:
                win = shift_ref[sh - 1, base:base + CONV_ROWS, :]
            term = win * w_ref[kk:kk + 1, :]
            acc = term if acc is None else acc + term
        dw = acc + b_ref[...]
        mu = jnp.mean(dw, axis=-1, keepdims=True)
        cen = dw - mu
        var = jnp.mean(cen * cen, axis=-1, keepdims=True)
        z = cen * lax.rsqrt(var + EPS) * g_ref[...] + beta_ref[...]
        o_ref[0, r0:r0 + CONV_ROWS, :] = (z * _sigmoid(z)).astype(BF16)


def _conv(u3, conv_w_p, conv_b, ln_g, ln_b):
    b, s, c = u3.shape
    ts = TS_CONV
    hb = ts // CONV_HALO
    return pl.pallas_call(
        _conv_kernel,
        grid=(b, s // ts),
        in_specs=[pl.BlockSpec((1, CONV_HALO, c), lambda bi, j: (bi, jnp.maximum(j * hb - 1, 0), 0)),
                  pl.BlockSpec((1, ts, c), lambda bi, j: (bi, j, 0)),
                  pl.BlockSpec(conv_w_p.shape, lambda bi, j: (0, 0)),
                  pl.BlockSpec((1, c), lambda bi, j: (0, 0)),
                  pl.BlockSpec((1, c), lambda bi, j: (0, 0)),
                  pl.BlockSpec((1, c), lambda bi, j: (0, 0))],
        out_specs=pl.BlockSpec((1, ts, c), lambda bi, j: (bi, j, 0)),
        out_shape=jax.ShapeDtypeStruct((b, s, c), BF16),
        scratch_shapes=[pltpu.VMEM((CONV_HALO + ts, c), F32),
                        pltpu.VMEM((SUBLANES - 1, CONV_HALO + ts - SUBLANES, c), F32)],
        compiler_params=_cparams(("arbitrary", "arbitrary")),
        name="conv",
    )(u3, u3, conv_w_p, conv_b, ln_g, ln_b)


def _attn_kernel(qt_ref, k_ref, vt_ref, o_ref, sa_ref, sb_ref):
    qi = pl.program_id(2)
    tq = qt_ref.shape[3]
    heads = qt_ref.shape[1]
    key_chunk = lax.broadcasted_iota(jnp.int32, (tq, tq), 0) // CHUNK
    qry_chunk = lax.broadcasted_iota(jnp.int32, (tq, tq), 1) // CHUNK
    diag_mask = key_chunk <= qry_chunk
    ones_rows = (lax.broadcasted_iota(jnp.int32, (DENOM_ROWS, tq), 0) == 0).astype(BF16)

    def scores(jt, s_ref):
        start = pl.multiple_of(jt * tq, tq)
        for hh in range(heads):
            kj = k_ref[0, hh, pl.ds(start, tq), :]
            s_ref[hh] = jnp.dot(kj, qt_ref[0, hh], preferred_element_type=F32)

    def consume(jt, s_ref, carry, masked):
        start = pl.multiple_of(jt * tq, tq)
        new = []
        for hh in range(heads):
            m, acc = carry[hh]
            st = s_ref[hh]
            if masked:
                st = jnp.where(diag_mask, st, -jnp.inf)
            vj = vt_ref[0, hh * V_HEAD_DIM:(hh + 1) * V_HEAD_DIM, pl.ds(start, tq)]
            m_new = jnp.maximum(m, jnp.max(st, axis=0, keepdims=True))
            alpha = jnp.exp2(m - m_new)
            p = jnp.exp2(st - m_new)
            v_ext = jnp.concatenate([vj, ones_rows], axis=0)
            acc_new = alpha * acc + jnp.dot(v_ext, p.astype(BF16), preferred_element_type=F32)
            new.append((m_new, acc_new))
        return tuple(new)

    def body(u, carry):
        scores(2 * u + 1, sb_ref)
        carry = consume(2 * u, sa_ref, carry, False)
        scores(2 * u + 2, sa_ref)
        carry = consume(2 * u + 1, sb_ref, carry, False)
        return carry

    init = tuple((jnp.full((1, tq), -jnp.inf, F32), jnp.zeros((V_HEAD_DIM + DENOM_ROWS, tq), F32))
                 for _ in range(heads))
    scores(0, sa_ref)
    carry = lax.fori_loop(0, qi // 2, body, init)

    def last_even(carry):
        return consume(qi, sa_ref, carry, True)

    def last_odd(carry):
        scores(qi, sb_ref)
        carry = consume(qi - 1, sa_ref, carry, False)
        return consume(qi, sb_ref, carry, True)

    carry = lax.cond(qi % 2 == 1, last_odd, last_even, carry)
    out_t = jnp.concatenate([acc[:V_HEAD_DIM] / acc[V_HEAD_DIM:V_HEAD_DIM + 1] for (_, acc) in carry],
                            axis=0)
    o_ref[0] = out_t.T.astype(BF16)


def _attn(qt4, k4, vt3):
    b, nh, s, _ = k4.shape
    tq = TQ_ATTN
    hps = HEADS_PER_STEP
    return pl.pallas_call(
        _attn_kernel,
        scratch_shapes=[pltpu.VMEM((hps, tq, tq), F32), pltpu.VMEM((hps, tq, tq), F32)],
        grid=(b, nh // hps, s // tq),
        in_specs=[pl.BlockSpec((1, hps, LANES, tq), lambda bi, hp, qi: (bi, hp, 0, qi)),
                  pl.BlockSpec((1, hps, s, LANES), lambda bi, hp, qi: (bi, hp, 0, 0)),
                  pl.BlockSpec((1, hps * V_HEAD_DIM, s), lambda bi, hp, qi: (bi, hp, 0))],
        out_specs=pl.BlockSpec((1, tq, hps * V_HEAD_DIM), lambda bi, hp, qi: (bi, qi, hp)),
        out_shape=jax.ShapeDtypeStruct((b, s, nh * V_HEAD_DIM), BF16),
        compiler_params=_cparams(("arbitrary", "arbitrary", "arbitrary")),
        name="attn",
    )(qt4, k4, vt3)


def _ffn_input(x1, nffn, mod_ref):
    sh_f = mod_ref[0, 3:4, :]
    sc_f = mod_ref[0, 4:5, :]
    y = x1 * lax.rsqrt(jnp.mean(x1 * x1, axis=-1, keepdims=True) + EPS) * nffn
    return y * (1.0 + sc_f) + sh_f


def _outproj_kernel(yc_ref, ya_ref, x_ref, mod_ref, wout_ref, nffn_ref, wr_ref, br_ref,
                    x1_ref, route_ref):
    tm = x_ref.shape[0]
    half = tm // 2
    mixed = [jnp.dot(jnp.concatenate([yc_ref[r0:r0 + half, :], ya_ref[r0:r0 + half, :]], axis=-1),
                     wout_ref[...], preferred_element_type=F32) for r0 in (0, half)]
    for hb, r0 in enumerate((0, half)):
        _outproj_rows(r0, half, mixed[hb], x_ref, mod_ref, nffn_ref, wr_ref, br_ref, x1_ref, route_ref)


def _outproj_rows(r0, rows, mixed, x_ref, mod_ref, nffn_ref, wr_ref, br_ref, x1_ref, route_ref):
    g_a = mod_ref[0, 2:3, :]
    x1 = x_ref[r0:r0 + rows, :] + g_a * mixed
    x1_ref[r0:r0 + rows, :] = x1
    h2 = _ffn_input(x1, nffn_ref[...], mod_ref)
    logits = jnp.dot(h2.astype(BF16), wr_ref[...], preferred_element_type=F32) + br_ref[...]
    lane = lax.broadcasted_iota(jnp.int32, logits.shape, 1)
    lane_f = lane.astype(F32)
    big = float(LANES)
    is_g = (lane >= N_EXPERTS) & (lane < N_EXPERTS + N_GROUPS)
    gl = jnp.where(is_g, logits, -jnp.inf)
    gmax = jnp.max(gl, axis=-1, keepdims=True)
    g_top = 1.0 / jnp.sum(jnp.exp(gl - gmax), axis=-1, keepdims=True)
    g_idx = jnp.min(jnp.where(gl == gmax, lane_f - N_EXPERTS, big), axis=-1, keepdims=True)
    in_grp = (lane < N_EXPERTS) & ((lane // EXPERTS_PER_GROUP).astype(F32) == g_idx)
    el = jnp.where(in_grp, logits, -jnp.inf)
    e1 = jnp.max(el, axis=-1, keepdims=True)
    i1 = jnp.min(jnp.where(el == e1, lane_f, big), axis=-1, keepdims=True)
    el2 = jnp.where(lane_f == i1, -jnp.inf, el)
    e2 = jnp.max(el2, axis=-1, keepdims=True)
    i2 = jnp.min(jnp.where(el2 == e2, lane_f, big), axis=-1, keepdims=True)
    t = jnp.exp(e2 - e1)
    w1 = g_top / (1.0 + t)
    w2 = g_top * t / (1.0 + t)
    route_ref[r0:r0 + rows, :] = jnp.where(lane == 0, w1, jnp.where(lane == 1, w2, jnp.where(
        lane == 2, i1, jnp.where(lane == 3, i2, 0.0))))


def _outproj(yc, ya, x2, mod3, wout, nffn, wr, br, s):
    n, d = x2.shape
    tm = TM_PROJ
    tpb = s // tm
    c = yc.shape[1]
    full = lambda a: pl.BlockSpec(a.shape, lambda i: (0,) * a.ndim)
    return pl.pallas_call(
        _outproj_kernel,
        grid=(n // tm,),
        in_specs=[pl.BlockSpec((tm, c), lambda i: (i, 0)),
                  pl.BlockSpec((tm, c), lambda i: (i, 0)),
                  pl.BlockSpec((tm, d), lambda i: (i, 0)),
                  pl.BlockSpec((1, 6, d), lambda i: (i // tpb, 0, 0)),
                  full(wout), full(nffn), full(wr), full(br)],
        out_specs=[pl.BlockSpec((tm, d), lambda i: (i, 0)),
                   pl.BlockSpec((tm, LANES), lambda i: (i, 0))],
        out_shape=[jax.ShapeDtypeStruct((n, d), F32),
                   jax.ShapeDtypeStruct((n, LANES), F32)],
        compiler_params=_cparams(("arbitrary",)),
        name="outproj",
    )(yc, ya, x2, mod3, wout, nffn, wr, br)


def _store_token_tiles(ref, row0, val, lead=()):
    rows = val.shape[0]
    for cc in range(TOKEN_ROWS):
        ref[lead + (pl.ds(row0 * TOKEN_PITCH + cc, rows, stride=TOKEN_PITCH), slice(None))] = (
            val[:, cc * LANES:(cc + 1) * LANES])


def _load_token_tiles(ref, row0, rows, lead=()):
    parts = [ref[lead + (pl.ds(row0 * TOKEN_PITCH + cc, rows, stride=TOKEN_PITCH), slice(None))]
             for cc in range(TOKEN_ROWS)]
    return jnp.concatenate(parts, axis=-1)


def _zero_spare_rows(ref, rows, lead=()):
    ref[lead + (pl.ds(TOKEN_ROWS, rows, stride=TOKEN_PITCH), slice(None))] = jnp.zeros((rows, LANES), F32)


def _idx_slot(ismem, s):
    return ismem.at[pl.ds(pl.multiple_of(s * IDX_SLOT, IDX_SLOT), IDX_SLOT)]


def _gather_rows_step(i, n_steps, idx_hbm, src_hbm, ismem, buf, isem, rsem):
    n_rows = IDX_SLOT
    slot = i % 2
    nslot = 1 - slot

    def idx_copy(step, s):
        return pltpu.make_async_copy(idx_hbm.at[step], _idx_slot(ismem, s), isem.at[s])

    def row_copy(row0, r, s):
        return pltpu.make_async_copy(src_hbm.at[pl.ds(row0, TOKEN_ROWS)],
                                     buf.at[s, pl.ds(r * TOKEN_PITCH, TOKEN_ROWS)], rsem.at[s])

    def issue_rows(s):
        def body(r, carry):
            row_copy(ismem[s * IDX_SLOT + r], r, s).start()
            return carry
        lax.fori_loop(0, n_rows, body, 0, unroll=8)

    def wait_rows(s):
        def body(r, carry):
            row_copy(0, r, s).wait()
            return carry
        lax.fori_loop(0, n_rows, body, 0, unroll=8)

    @pl.when(i == 0)
    def _():
        idx_copy(0, 0).start()
        idx_copy(0, 0).wait()
        issue_rows(0)

        @pl.when(n_steps > 1)
        def _():
            idx_copy(1, 1).start()

    @pl.when(i + 1 < n_steps)
    def _():
        idx_copy(i + 1, nslot).wait()
        issue_rows(nslot)

    @pl.when(i + 2 < n_steps)
    def _():
        idx_copy(i + 2, slot).start()

    wait_rows(slot)


def _dispatch_kernel(pad_row_ref, pad_cnt_ref, nv_ref, pos_hbm, x1_ref, mod_ref, nffn_ref, xs_hbm,
                     ismem, hbuf, zero_ref, isem, csem, zsem):
    i = pl.program_id(0)
    n_steps = pl.num_programs(0)
    n_pairs = IDX_SLOT
    tb = n_pairs // TOP_K
    slot = i % 2

    def idx_copy(step, s):
        return pltpu.make_async_copy(pos_hbm.at[step], _idx_slot(ismem, s), isem.at[s])

    def tile_copy(r, row0, s):
        return pltpu.make_async_copy(hbuf.at[s, pl.ds(r * TOKEN_PITCH, TOKEN_PITCH)],
                                     xs_hbm.at[pl.ds(row0, TOKEN_PITCH)], csem.at[s])

    def zero_copy(row0):
        return pltpu.make_async_copy(zero_ref.at[pl.ds(0, TOKEN_PITCH)],
                                     xs_hbm.at[pl.ds(row0, TOKEN_PITCH)], zsem.at[0])

    def for_pad_tiles(fn):
        def per_expert(e, carry):
            def per_tile(j, c2):
                fn(pad_row_ref[e] + j * TOKEN_PITCH)
                return c2
            return lax.fori_loop(0, pad_cnt_ref[e], per_tile, carry)
        lax.fori_loop(0, N_EXPERTS, per_expert, 0)

    tile_rows = zero_ref.shape[0]
    n_tiles = xs_hbm.shape[0] // tile_rows

    def tail_copy(j):
        return pltpu.make_async_copy(
            zero_ref, xs_hbm.at[pl.ds(pl.multiple_of(j * tile_rows, SUBLANES), tile_rows)], zsem.at[1])

    def for_tail_tiles(fn):
        def body(j, carry):
            fn(j)
            return carry
        lax.fori_loop(nv_ref[0], n_tiles, body, 0)

    @pl.when(i == 0)
    def _():
        idx_copy(0, 0).start()
        zero_ref[...] = jnp.zeros(zero_ref.shape, F32)
        for sl in range(2):
            _zero_spare_rows(hbuf, tb, (sl,))
        for_pad_tiles(lambda row0: zero_copy(row0).start())
        for_tail_tiles(lambda j: tail_copy(j).start())

    idx_copy(i, slot).wait()

    @pl.when(i + 1 < n_steps)
    def _():
        idx_copy(i + 1, 1 - slot).start()

    _store_token_tiles(hbuf, 0, _ffn_input(x1_ref[...], nffn_ref[...], mod_ref), (slot,))

    def issue(r, carry):
        for kk in range(TOP_K):
            tile_copy(r, ismem[slot * IDX_SLOT + kk * tb + r], slot).start()
        return carry
    lax.fori_loop(0, tb, issue, 0, unroll=4)

    def drain(s):
        def body(r, carry):
            tile_copy(0, 0, s).wait()
            return carry
        lax.fori_loop(0, n_pairs, body, 0, unroll=8)

    @pl.when(i > 0)
    def _():
        drain(1 - slot)

    @pl.when(i == n_steps - 1)
    def _():
        drain(slot)
        for_pad_tiles(lambda row8: zero_copy(0).wait())
        for_tail_tiles(lambda j: tail_copy(0).wait())


def _dispatch(pad_row8, pad_cnt, n_valid, pos8_blocks, x1, mod3, nffn, n_rows, s):
    n_blocks, n_pairs = pos8_blocks.shape
    n, d = x1.shape
    tb = n_pairs // TOP_K
    tpb = s // tb
    grid_spec = pltpu.PrefetchScalarGridSpec(
        num_scalar_prefetch=3,
        grid=(n_blocks,),
        in_specs=[pl.BlockSpec(memory_space=pl.ANY),
                  pl.BlockSpec((tb, d), lambda i, pr, pc, nv: (i, 0)),
                  pl.BlockSpec((1, 6, d), lambda i, pr, pc, nv: (i // tpb, 0, 0)),
                  pl.BlockSpec((1, d), lambda i, pr, pc, nv: (0, 0))],
        out_specs=pl.BlockSpec(memory_space=pl.ANY),
        scratch_shapes=[pltpu.SMEM((2 * IDX_SLOT,), jnp.int32),
                        pltpu.VMEM((2, tb * TOKEN_PITCH, LANES), F32),
                        pltpu.VMEM((T_MOE * TOKEN_PITCH, LANES), F32),
                        pltpu.SemaphoreType.DMA((2,)),
                        pltpu.SemaphoreType.DMA((2,)),
                        pltpu.SemaphoreType.DMA((2,))])
    return pl.pallas_call(
        _dispatch_kernel,
        grid_spec=grid_spec,
        out_shape=jax.ShapeDtypeStruct((n_rows * TOKEN_PITCH, LANES), F32),
        compiler_params=_cparams(("arbitrary",)),
        name="dispatch",
    )(pad_row8, pad_cnt, n_valid, pos8_blocks, x1, mod3, nffn)


def _moe_kernel(te_ref, nv_ref, xs_ref, wg_f32, wu_f32, wd_f32, y_ref, wg_ref, wu_ref, wd_ref):
    i = pl.program_id(0)
    t = xs_ref.shape[0] // TOKEN_PITCH

    @pl.when((i == 0) | (te_ref[i] != te_ref[jnp.maximum(i - 1, 0)]))
    def _():
        wg_ref[...] = wg_f32[...].astype(BF16)
        wu_ref[...] = wu_f32[...].astype(BF16)
        wd_ref[...] = wd_f32[...].astype(BF16)

    @pl.when(i < nv_ref[0])
    def _():
        half = t // 2
        acts = []
        for hb in range(2):
            xg = _load_token_tiles(xs_ref, hb * half, half).astype(BF16)
            g = jnp.dot(xg, wg_ref[0], preferred_element_type=F32)
            u = jnp.dot(xg, wu_ref[0], preferred_element_type=F32)
            acts.append((g, u))
        for hb in range(2):
            g, u = acts[hb]
            a = (g * _sigmoid(g) * u).astype(BF16)
            _store_token_tiles(y_ref, hb * half, jnp.dot(a, wd_ref[0], preferred_element_type=F32))
        _zero_spare_rows(y_ref, t)

    @pl.when(i >= nv_ref[0])
    def _():
        y_ref[...] = jnp.zeros(y_ref.shape, F32)


def _moe(tile_expert, n_valid, xs, wg, wu, wd):
    t = T_MOE
    n_tiles = xs.shape[0] // (t * TOKEN_PITCH)
    d = wg.shape[1]
    de = wg.shape[2]
    grid_spec = pltpu.PrefetchScalarGridSpec(
        num_scalar_prefetch=2,
        grid=(n_tiles,),
        in_specs=[pl.BlockSpec((t * TOKEN_PITCH, LANES), lambda i, te, nv: (jnp.minimum(i, nv[0] - 1), 0)),
                  pl.BlockSpec((1, d, de), lambda i, te, nv: (te[i], 0, 0)),
                  pl.BlockSpec((1, d, de), lambda i, te, nv: (te[i], 0, 0)),
                  pl.BlockSpec((1, de, d), lambda i, te, nv: (te[i], 0, 0))],
        out_specs=pl.BlockSpec((t * TOKEN_PITCH, LANES), lambda i, te, nv: (i, 0)),
        scratch_shapes=[pltpu.VMEM((1, d, de), BF16), pltpu.VMEM((1, d, de), BF16),
                        pltpu.VMEM((1, de, d), BF16)])
    return pl.pallas_call(
        _moe_kernel,
        grid_spec=grid_spec,
        out_shape=jax.ShapeDtypeStruct(xs.shape, F32),
        compiler_params=_cparams(("arbitrary",)),
        name="moe",
    )(tile_expert, n_valid, xs, wg, wu, wd)


def _combine_kernel(pos_hbm, y_hbm, x1_ref, route_ref, mod_ref, o_ref, ismem, ybuf, isem, rsem):
    i = pl.program_id(0)
    tm = x1_ref.shape[0]
    _gather_rows_step(i, pl.num_programs(0), pos_hbm, y_hbm, ismem, ybuf, isem, rsem)
    slot = i % 2
    g_f = mod_ref[0, 5:6, :]
    w0 = route_ref[:, 0:1]
    w1 = route_ref[:, 1:2]
    y = (w0 * _load_token_tiles(ybuf, 0, tm, (slot,)) + w1 * _load_token_tiles(ybuf, tm, tm, (slot,)))
    o_ref[...] = x1_ref[...] + g_f * y


def _combine(pos2, y_sorted, x1, route, mod3, s):
    n, d = x1.shape
    tm = TM_COMB
    tpb = s // tm
    return pl.pallas_call(
        _combine_kernel,
        grid=(n // tm,),
        in_specs=[pl.BlockSpec(memory_space=pl.ANY),
                  pl.BlockSpec(memory_space=pl.ANY),
                  pl.BlockSpec((tm, d), lambda i: (i, 0)),
                  pl.BlockSpec((tm, LANES), lambda i: (i, 0)),
                  pl.BlockSpec((1, 6, d), lambda i: (i // tpb, 0, 0))],
        out_specs=pl.BlockSpec((tm, d), lambda i: (i, 0)),
        out_shape=jax.ShapeDtypeStruct((n, d), F32),
        scratch_shapes=[pltpu.SMEM((2 * IDX_SLOT,), jnp.int32),
                        pltpu.VMEM((2, TOP_K * tm * TOKEN_PITCH, LANES), F32),
                        pltpu.SemaphoreType.DMA((2,)),
                        pltpu.SemaphoreType.DMA((2,))],
        compiler_params=_cparams(("arbitrary",)),
        name="combine",
    )(pos2, y_sorted, x1, route, mod3)


def _routing_tables(route, n):
    t = T_MOE
    flat_e = route[:, 2:4].astype(jnp.int32).reshape(-1)
    onehot = (flat_e[:, None] == jnp.arange(N_EXPERTS, dtype=jnp.int32)[None, :]).astype(jnp.int32)
    csum = jnp.cumsum(onehot, axis=0)
    rank = jnp.sum(onehot * csum, axis=1) - 1
    counts = csum[-1]
    ptiles = (counts + t - 1) // t
    tile_end = jnp.cumsum(ptiles)
    tile_start = tile_end - ptiles
    pos = jnp.sum(onehot * (tile_start * t)[None, :], axis=1) + rank
    n_tiles = (TOP_K * n) // t + N_EXPERTS
    n_valid = tile_end[-1]
    tile_ids = jnp.arange(n_tiles, dtype=jnp.int32)
    te = jnp.sum((tile_ids[:, None] >= tile_end[None, :]).astype(jnp.int32), axis=1)
    te_last = jnp.sum((n_valid - 1 >= tile_end).astype(jnp.int32))
    tile_expert = jnp.minimum(jnp.where(tile_ids < n_valid, te, te_last), N_EXPERTS - 1)
    pos8 = (pos * TOKEN_PITCH).astype(jnp.int32).reshape(n, TOP_K)
    blocks = lambda tb: pos8.reshape(n // tb, tb, TOP_K).transpose(0, 2, 1).reshape(n // tb, TOP_K * tb)
    pad_row8 = ((tile_start * t + counts) * TOKEN_PITCH).astype(jnp.int32)
    pad_cnt = (ptiles * t - counts).astype(jnp.int32)
    return (tile_expert.astype(jnp.int32), n_valid.astype(jnp.int32).reshape(1), pad_row8, pad_cnt,
            blocks(TB_DISPATCH), blocks(TM_COMB), n_tiles * t)


def _pad_lanes(a, left, total):
    pad = [(0, 0)] * (a.ndim - 1) + [(left, total - left - a.shape[-1])]
    return jnp.pad(a, pad)


def _layer(x2, c, cos_t, sin_t, b, s, w_ada, b_ada, norm_mix, w_in, conv_w, conv_b, conv_ln_g, conv_ln_b,
           q_a_norm, w_q_b, kv_a_norm, w_kv_b, q_norm, k_norm, w_out, norm_ffn, w_group, b_group,
           w_expert, b_expert, w_gate_e, w_up_e, w_down_e):
    n, d = x2.shape
    c_conv = conv_w.shape[1]
    q_rank = q_a_norm.shape[0]
    kv_rank = kv_a_norm.shape[0]
    row = lambda a: a.reshape(1, -1)

    o1 = 2 * c_conv
    o2 = o1 + q_rank
    o3 = o2 + kv_rank
    win_p = jnp.concatenate([w_in[:, :o3], _pad_lanes(w_in[:, o3:], QK_NOPE_DIM, LANES)], axis=1).astype(BF16)
    wq_p = _pad_lanes(w_q_b.reshape(q_rank, N_HEADS, QK_HEAD_DIM), 0, LANES).reshape(q_rank, N_HEADS * LANES)
    wkv = w_kv_b.reshape(kv_rank, N_HEADS, QK_NOPE_DIM + V_HEAD_DIM)
    wk_p = _pad_lanes(wkv[..., :QK_NOPE_DIM], 0, LANES).reshape(kv_rank, N_HEADS * LANES)
    wv = wkv[..., QK_NOPE_DIM:].reshape(kv_rank, N_HEADS * V_HEAD_DIM)
    gain_t = lambda g: jnp.broadcast_to(_pad_lanes(row(g), 0, LANES).reshape(LANES, 1), (LANES, TM_PROJ))
    qn_t = gain_t(q_norm)
    kn_t = gain_t(k_norm)
    wr = _pad_lanes(jnp.concatenate([w_expert, w_group], axis=1), 0, LANES).astype(BF16)
    br = _pad_lanes(row(jnp.concatenate([b_expert, b_group])), 0, LANES)
    conv_w_p = jnp.pad(conv_w, ((0, CONV_HALO - CONV_K), (0, 0)))

    mod3 = _ada(c, w_ada, b_ada).reshape(b, 6, d)
    u, qt4, k4, vt3 = _inproj(x2, mod3, row(norm_mix), win_p, row(q_a_norm), wq_p.T.astype(BF16),
                              row(kv_a_norm), wk_p.T.astype(BF16), wv.T.astype(BF16), qn_t, kn_t,
                              cos_t, sin_t, b, s)
    y_conv = _conv(u.reshape(b, s, c_conv), conv_w_p, row(conv_b), row(conv_ln_g), row(conv_ln_b))
    y_attn = _attn(qt4, k4, vt3)
    x1, route = _outproj(y_conv.reshape(n, c_conv), y_attn.reshape(n, -1), x2, mod3,
                         w_out.astype(BF16), row(norm_ffn), wr, br, s)
    tile_expert, n_valid, pad_row8, pad_cnt, pos8_disp, pos8_comb, n_rows = _routing_tables(route, n)
    xs = _dispatch(pad_row8, pad_cnt, n_valid, pos8_disp, x1, mod3, row(norm_ffn), n_rows, s)
    ys = _moe(tile_expert, n_valid, xs, w_gate_e, w_up_e, w_down_e)
    return _combine(pos8_comb, ys, x1, route, mod3, s)


def kernel(x, c, positions, w_ada, b_ada, norm_mix, w_in, conv_w, conv_b, conv_ln_g, conv_ln_b, q_a_norm,
           w_q_b, kv_a_norm, w_kv_b, q_norm, k_norm, w_out, norm_ffn, w_group, b_group, w_expert, b_expert,
           w_gate_e, w_up_e, w_down_e):
    b, s, d = x.shape
    n = b * s
    inv_freq = ROPE_THETA ** (-jnp.arange(0, QK_ROPE_DIM, 2, dtype=F32) / QK_ROPE_DIM)
    ang = inv_freq[:, None] * positions.astype(F32).reshape(1, n)
    cos_t, sin_t = jnp.cos(ang), jnp.sin(ang)
    x2 = x.reshape(n, d)
    for l in range(w_ada.shape[0]):
        x2 = _layer(x2, c, cos_t, sin_t, b, s, w_ada[l], b_ada[l], norm_mix[l], w_in[l], conv_w[l], conv_b[l],
                    conv_ln_g[l], conv_ln_b[l], q_a_norm[l], w_q_b[l], kv_a_norm[l], w_kv_b[l], q_norm[l],
                    k_norm[l], w_out[l], norm_ffn[l], w_group[l], b_group[l], w_expert[l], b_expert[l],
                    w_gate_e[l], w_up_e[l], w_down_e[l])
    return x2.reshape(b, s, d)
```

```python
import functools

import jax
import jax.numpy as jnp
from jax import lax
from jax.experimental import pallas as pl
from jax.experimental.pallas import tpu as pltpu

F32 = jnp.float32
BF16 = jnp.bfloat16

CHUNK = 64
CONV_K = 31
N_HEADS = 8
V_HEAD_DIM = 64
QK_NOPE_DIM = 64
QK_ROPE_DIM = 32
QK_HEAD_DIM = QK_NOPE_DIM + QK_ROPE_DIM
N_GROUPS = 4
EXPERTS_PER_GROUP = 8
N_EXPERTS = N_GROUPS * EXPERTS_PER_GROUP
TOP_K = 2
ROPE_THETA = 10000.0
EPS = 1e-6
LOG2_E = 1.4426950408889634

LANES = 128
SUBLANES = 8
TOKEN_ROWS = 8
TOKEN_PITCH = TOKEN_ROWS + 1
CONV_HALO = 32

TM_PROJ = 512
TS_CONV = 256
CONV_ROWS = 32
TQ_ATTN = 512
DENOM_ROWS = 16
HEADS_PER_STEP = 4
T_MOE = 256
IDX_SLOT = 1024
DMA_UNROLL = 32
TB_DISPATCH = IDX_SLOT // TOP_K
TM_COMB = IDX_SLOT // TOP_K
VMEM_LIMIT = 48 * 1024 * 1024


def _sigmoid(v):
    return 1.0 / (1.0 + jnp.exp(-v))


def _cparams(sem):
    return pltpu.CompilerParams(dimension_semantics=sem, vmem_limit_bytes=VMEM_LIMIT)


def _ada_kernel(c_ref, w_ref, b_ref, o_ref):
    c = c_ref[...]
    s = c * _sigmoid(c)
    o_ref[...] = jnp.dot(s, w_ref[...], preferred_element_type=F32,
                         precision=lax.Precision.HIGHEST) + b_ref[...]


def _ada(c, w_ada, b_ada):
    b, d = c.shape
    n_out = w_ada.shape[1]
    return pl.pallas_call(
        _ada_kernel,
        grid=(n_out // d,),
        in_specs=[pl.BlockSpec((b, d), lambda j: (0, 0)),
                  pl.BlockSpec((d, d), lambda j: (0, j)),
                  pl.BlockSpec((1, d), lambda j: (0, j))],
        out_specs=pl.BlockSpec((b, d), lambda j: (0, j)),
        out_shape=jax.ShapeDtypeStruct((b, n_out), F32),
        compiler_params=_cparams(("arbitrary",)),
        name="ada",
    )(c, w_ada, b_ada.reshape(1, n_out))


def _head_norm_rope_t(tt, gain, cos_t, sin_t):
    live = tt[:QK_HEAD_DIM]
    r = lax.rsqrt(jnp.sum(live * live, axis=0, keepdims=True) * (1.0 / QK_HEAD_DIM) + EPS)
    tn = live * r * gain[:QK_HEAD_DIM]
    half = QK_ROPE_DIM // 2
    lo = tn[QK_NOPE_DIM:QK_NOPE_DIM + half]
    hi = tn[QK_NOPE_DIM + half:QK_HEAD_DIM]
    return jnp.concatenate([tn[:QK_NOPE_DIM], lo * cos_t - hi * sin_t, hi * cos_t + lo * sin_t,
                            jnp.zeros((LANES - QK_HEAD_DIM, tt.shape[1]), F32)], axis=0)


def _inproj_kernel(x_ref, mod_ref, nmix_ref, win_ref, qan_ref, wqt_ref, kvan_ref, wkt_ref, wvt_ref,
                   qn_ref, kn_ref, cos_ref, sin_ref, u_ref, qt_ref, k_ref, vt_ref):
    x = x_ref[...]
    sh_a = mod_ref[0, 0:1, :]
    sc_a = mod_ref[0, 1:2, :]
    y = x * lax.rsqrt(jnp.mean(x * x, axis=-1, keepdims=True) + EPS) * nmix_ref[...]
    h = y * (1.0 + sc_a) + sh_a
    proj = jnp.dot(h.astype(BF16), win_ref[...], preferred_element_type=F32)
    c_conv = u_ref.shape[-1]
    val = proj[:, :c_conv]
    gate = proj[:, c_conv:2 * c_conv]
    u_ref[...] = (val * _sigmoid(gate)).astype(BF16)
    o = 2 * c_conv
    q_rank = wqt_ref.shape[1]
    kv_rank = wkt_ref.shape[1]
    cq = proj[:, o:o + q_rank]
    ckv = proj[:, o + q_rank:o + q_rank + kv_rank]
    kr = proj[:, o + q_rank + kv_rank:o + q_rank + kv_rank + LANES]
    cqn = cq * lax.rsqrt(jnp.mean(cq * cq, axis=-1, keepdims=True) + EPS) * qan_ref[...]
    ckvn = ckv * lax.rsqrt(jnp.mean(ckv * ckv, axis=-1, keepdims=True) + EPS) * kvan_ref[...]
    cqn_t = cqn.T.astype(BF16)
    ckvn_t = ckvn.T.astype(BF16)
    kr_t = kr.T
    qt_all = jnp.dot(wqt_ref[...], cqn_t, preferred_element_type=F32)
    kt_all = jnp.dot(wkt_ref[...], ckvn_t, preferred_element_type=F32)
    vt_ref[0] = jnp.dot(wvt_ref[...], ckvn_t, preferred_element_type=F32).astype(BF16)
    cos_t = cos_ref[...]
    sin_t = sin_ref[...]
    scale = QK_HEAD_DIM ** -0.5 * LOG2_E
    for hd in range(N_HEADS):
        qh = qt_all[hd * LANES:(hd + 1) * LANES]
        qt_ref[0, hd] = (_head_norm_rope_t(qh, qn_ref[...], cos_t, sin_t) * scale).astype(BF16)
        kh = kt_all[hd * LANES:(hd + 1) * LANES] + kr_t
        k_ref[0, hd] = _head_norm_rope_t(kh, kn_ref[...], cos_t, sin_t).T.astype(BF16)


def _inproj(x2, mod3, nmix, win_p, qan, wqt_p, kvan, wkt_p, wvt, qn_t, kn_t, cos_t, sin_t, b, s):
    n, d = x2.shape
    tm = TM_PROJ
    tpb = s // tm
    c_conv = (win_p.shape[1] - wqt_p.shape[1] - wkt_p.shape[1] - LANES) // 2
    half = QK_ROPE_DIM // 2
    full = lambda a: pl.BlockSpec(a.shape, lambda i: (0,) * a.ndim)
    return pl.pallas_call(
        _inproj_kernel,
        grid=(n // tm,),
        in_specs=[pl.BlockSpec((tm, d), lambda i: (i, 0)),
                  pl.BlockSpec((1, 6, d), lambda i: (i // tpb, 0, 0)),
                  full(nmix), full(win_p), full(qan), full(wqt_p), full(kvan), full(wkt_p), full(wvt),
                  full(qn_t), full(kn_t),
                  pl.BlockSpec((half, tm), lambda i: (0, i)),
                  pl.BlockSpec((half, tm), lambda i: (0, i))],
        out_specs=[pl.BlockSpec((tm, c_conv), lambda i: (i, 0)),
                   pl.BlockSpec((1, N_HEADS, LANES, tm), lambda i: (i // tpb, 0, 0, i % tpb)),
                   pl.BlockSpec((1, N_HEADS, tm, LANES), lambda i: (i // tpb, 0, i % tpb, 0)),
                   pl.BlockSpec((1, c_conv, tm), lambda i: (i // tpb, 0, i % tpb))],
        out_shape=[jax.ShapeDtypeStruct((n, c_conv), BF16),
                   jax.ShapeDtypeStruct((b, N_HEADS, LANES, s), BF16),
                   jax.ShapeDtypeStruct((b, N_HEADS, s, LANES), BF16),
                   jax.ShapeDtypeStruct((b, N_HEADS * V_HEAD_DIM, s), BF16)],
        compiler_params=_cparams(("arbitrary",)),
        name="inproj",
    )(x2, mod3, nmix, win_p, qan, wqt_p, kvan, wkt_p, wvt, qn_t, kn_t, cos_t, sin_t)


def _conv_kernel(prev_ref, cur_ref, w_ref, b_ref, g_ref, beta_ref, o_ref, buf_ref, shift_ref):
    j = pl.program_id(1)
    ts = cur_ref.shape[1]

    @pl.when(j == 0)
    def _():
        buf_ref[0:CONV_HALO, :] = jnp.zeros((CONV_HALO, buf_ref.shape[1]), F32)

    @pl.when(j > 0)
    def _():
        buf_ref[0:CONV_HALO, :] = prev_ref[0].astype(F32)

    buf_ref[CONV_HALO:CONV_HALO + ts, :] = cur_ref[0].astype(F32)
    span = CONV_HALO + ts - SUBLANES
    for sh in range(1, SUBLANES):
        shift_ref[sh - 1, 0:span, :] = buf_ref[sh:sh + span, :]
    first = CONV_HALO - (CONV_K - 1)
    for r0 in range(0, ts, CONV_ROWS):
        acc = None
        for kk in range(CONV_K):
            sh = (first + kk) % SUBLANES
            base = r0 + (first + kk) - sh
            if sh == 0:
                win = buf_ref[base:base + CONV_ROWS, :]
            else:
                win = shift_ref[sh - 1, base:base + CONV_ROWS, :]
            term = win * w_ref[kk:kk + 1, :]
            acc = term if acc is None else acc + term
        dw = acc + b_ref[...]
        mu = jnp.mean(dw, axis=-1, keepdims=True)
        cen = dw - mu
        var = jnp.mean(cen * cen, axis=-1, keepdims=True)
        z = cen * lax.rsqrt(var + EPS) * g_ref[...] + beta_ref[...]
        o_ref[0, r0:r0 + CONV_ROWS, :] = (z * _sigmoid(z)).astype(BF16)


def _conv(u3, conv_w_p, conv_b, ln_g, ln_b):
    b, s, c = u3.shape
    ts = TS_CONV
    hb = ts // CONV_HALO
    return pl.pallas_call(
        _conv_kernel,
        grid=(b, s // ts),
        in_specs=[pl.BlockSpec((1, CONV_HALO, c), lambda bi, j: (bi, jnp.maximum(j * hb - 1, 0), 0)),
                  pl.BlockSpec((1, ts, c), lambda bi, j: (bi, j, 0)),
                  pl.BlockSpec(conv_w_p.shape, lambda bi, j: (0, 0)),
                  pl.BlockSpec((1, c), lambda bi, j: (0, 0)),
                  pl.BlockSpec((1, c), lambda bi, j: (0, 0)),
                  pl.BlockSpec((1, c), lambda bi, j: (0, 0))],
        out_specs=pl.BlockSpec((1, ts, c), lambda bi, j: (bi, j, 0)),
        out_shape=jax.ShapeDtypeStruct((b, s, c), BF16),
        scratch_shapes=[pltpu.VMEM((CONV_HALO + ts, c), F32),
                        pltpu.VMEM((SUBLANES - 1, CONV_HALO + ts - SUBLANES, c), F32)],
        compiler_params=_cparams(("arbitrary", "arbitrary")),
        name="conv",
    )(u3, u3, conv_w_p, conv_b, ln_g, ln_b)


def _attn_kernel(qt_ref, k_ref, vt_ref, o_ref, sa_ref, sb_ref):
    qi = pl.program_id(2)
    tq = qt_ref.shape[3]
    heads = qt_ref.shape[1]
    key_chunk = lax.broadcasted_iota(jnp.int32, (tq, tq), 0) // CHUNK
    qry_chunk = lax.broadcasted_iota(jnp.int32, (tq, tq), 1) // CHUNK
    diag_mask = key_chunk <= qry_chunk
    ones_rows = (lax.broadcasted_iota(jnp.int32, (DENOM_ROWS, tq), 0) == 0).astype(BF16)

    def scores(jt, s_ref):
        start = pl.multiple_of(jt * tq, tq)
        for hh in range(heads):
            kj = k_ref[0, hh, pl.ds(start, tq), :]
            s_ref[hh] = jnp.dot(kj, qt_ref[0, hh], preferred_element_type=F32)

    def consume(jt, s_ref, carry, masked):
        start = pl.multiple_of(jt * tq, tq)
        new = []
        for hh in range(heads):
            m, acc = carry[hh]
            st = s_ref[hh]
            if masked:
                st = jnp.where(diag_mask, st, -jnp.inf)
            vj = vt_ref[0, hh * V_HEAD_DIM:(hh + 1) * V_HEAD_DIM, pl.ds(start, tq)]
            m_new = jnp.maximum(m, jnp.max(st, axis=0, keepdims=True))
            alpha = jnp.exp2(m - m_new)
            p = jnp.exp2(st - m_new)
            v_ext = jnp.concatenate([vj, ones_rows], axis=0)
            acc_new = alpha * acc + jnp.dot(v_ext, p.astype(BF16), preferred_element_type=F32)
            new.append((m_new, acc_new))
        return tuple(new)

    def body(u, carry):
        scores(2 * u + 1, sb_ref)
        carry = consume(2 * u, sa_ref, carry, False)
        scores(2 * u + 2, sa_ref)
        carry = consume(2 * u + 1, sb_ref, carry, False)
        return carry

    init = tuple((jnp.full((1, tq), -jnp.inf, F32), jnp.zeros((V_HEAD_DIM + DENOM_ROWS, tq), F32))
                 for _ in range(heads))
    scores(0, sa_ref)
    carry = lax.fori_loop(0, qi // 2, body, init)

    def last_even(carry):
        return consume(qi, sa_ref, carry, True)

    def last_odd(carry):
        scores(qi, sb_ref)
        carry = consume(qi - 1, sa_ref, carry, False)
        return consume(qi, sb_ref, carry, True)

    carry = lax.cond(qi % 2 == 1, last_odd, last_even, carry)
    out_t = jnp.concatenate([acc[:V_HEAD_DIM] / acc[V_HEAD_DIM:V_HEAD_DIM + 1] for (_, acc) in carry],
                            axis=0)
    o_ref[0] = out_t.T.astype(BF16)


def _attn(qt4, k4, vt3):
    b, nh, s, _ = k4.shape
    tq = TQ_ATTN
    hps = HEADS_PER_STEP
    return pl.pallas_call(
        _attn_kernel,
        scratch_shapes=[pltpu.VMEM((hps, tq, tq), F32), pltpu.VMEM((hps, tq, tq), F32)],
        grid=(b, nh // hps, s // tq),
        in_specs=[pl.BlockSpec((1, hps, LANES, tq), lambda bi, hp, qi: (bi, hp, 0, qi)),
                  pl.BlockSpec((1, hps, s, LANES), lambda bi, hp, qi: (bi, hp, 0, 0)),
                  pl.BlockSpec((1, hps * V_HEAD_DIM, s), lambda bi, hp, qi: (bi, hp, 0))],
        out_specs=pl.BlockSpec((1, tq, hps * V_HEAD_DIM), lambda bi, hp, qi: (bi, qi, hp)),
        out_shape=jax.ShapeDtypeStruct((b, s, nh * V_HEAD_DIM), BF16),
        compiler_params=_cparams(("arbitrary", "arbitrary", "arbitrary")),
        name="attn",
    )(qt4, k4, vt3)


def _ffn_input(x1, nffn, mod_ref):
    sh_f = mod_ref[0, 3:4, :]
    sc_f = mod_ref[0, 4:5, :]
    y = x1 * lax.rsqrt(jnp.mean(x1 * x1, axis=-1, keepdims=True) + EPS) * nffn
    return y * (1.0 + sc_f) + sh_f


def _outproj_kernel(yc_ref, ya_ref, x_ref, mod_ref, wout_ref, nffn_ref, wr_ref, br_ref,
                    x1_ref, route_ref):
    tm = x_ref.shape[0]
    half = tm // 2
    mixed = [jnp.dot(jnp.concatenate([yc_ref[r0:r0 + half, :], ya_ref[r0:r0 + half, :]], axis=-1),
                     wout_ref[...], preferred_element_type=F32) for r0 in (0, half)]
    for hb, r0 in enumerate((0, half)):
        _outproj_rows(r0, half, mixed[hb], x_ref, mod_ref, nffn_ref, wr_ref, br_ref, x1_ref, route_ref)


def _outproj_rows(r0, rows, mixed, x_ref, mod_ref, nffn_ref, wr_ref, br_ref, x1_ref, route_ref):
    g_a = mod_ref[0, 2:3, :]
    x1 = x_ref[r0:r0 + rows, :] + g_a * mixed
    x1_ref[r0:r0 + rows, :] = x1
    h2 = _ffn_input(x1, nffn_ref[...], mod_ref)
    logits = jnp.dot(h2.astype(BF16), wr_ref[...], preferred_element_type=F32) + br_ref[...]
    lane = lax.broadcasted_iota(jnp.int32, logits.shape, 1)
    lane_f = lane.astype(F32)
    big = float(LANES)
    is_g = (lane >= N_EXPERTS) & (lane < N_EXPERTS + N_GROUPS)
    gl = jnp.where(is_g, logits, -jnp.inf)
    gmax = jnp.max(gl, axis=-1, keepdims=True)
    g_top = 1.0 / jnp.sum(jnp.exp(gl - gmax), axis=-1, keepdims=True)
    g_idx = jnp.min(jnp.where(gl == gmax, lane_f - N_EXPERTS, big), axis=-1, keepdims=True)
    in_grp = (lane < N_EXPERTS) & ((lane // EXPERTS_PER_GROUP).astype(F32) == g_idx)
    el = jnp.where(in_grp, logits, -jnp.inf)
    e1 = jnp.max(el, axis=-1, keepdims=True)
    i1 = jnp.min(jnp.where(el == e1, lane_f, big), axis=-1, keepdims=True)
    el2 = jnp.where(lane_f == i1, -jnp.inf, el)
    e2 = jnp.max(el2, axis=-1, keepdims=True)
    i2 = jnp.min(jnp.where(el2 == e2, lane_f, big), axis=-1, keepdims=True)
    t = jnp.exp(e2 - e1)
    w1 = g_top / (1.0 + t)
    w2 = g_top * t / (1.0 + t)
    route_ref[r0:r0 + rows, :] = jnp.where(lane == 0, w1, jnp.where(lane == 1, w2, jnp.where(
        lane == 2, i1, jnp.where(lane == 3, i2, 0.0))))


def _outproj(yc, ya, x2, mod3, wout, nffn, wr, br, s):
    n, d = x2.shape
    tm = TM_PROJ
    tpb = s // tm
    c = yc.shape[1]
    full = lambda a: pl.BlockSpec(a.shape, lambda i: (0,) * a.ndim)
    return pl.pallas_call(
        _outproj_kernel,
        grid=(n // tm,),
        in_specs=[pl.BlockSpec((tm, c), lambda i: (i, 0)),
                  pl.BlockSpec((tm, c), lambda i: (i, 0)),
                  pl.BlockSpec((tm, d), lambda i: (i, 0)),
                  pl.BlockSpec((1, 6, d), lambda i: (i // tpb, 0, 0)),
                  full(wout), full(nffn), full(wr), full(br)],
        out_specs=[pl.BlockSpec((tm, d), lambda i: (i, 0)),
                   pl.BlockSpec((tm, LANES), lambda i: (i, 0))],
        out_shape=[jax.ShapeDtypeStruct((n, d), F32),
                   jax.ShapeDtypeStruct((n, LANES), F32)],
        compiler_params=_cparams(("arbitrary",)),
        name="outproj",
    )(yc, ya, x2, mod3, wout, nffn, wr, br)


def _store_token_tiles(ref, row0, val, lead=()):
    rows = val.shape[0]
    for cc in range(TOKEN_ROWS):
        ref[lead + (pl.ds(row0 * TOKEN_PITCH + cc, rows, stride=TOKEN_PITCH), slice(None))] = (
            val[:, cc * LANES:(cc + 1) * LANES])


def _load_token_tiles(ref, row0, rows, lead=()):
    parts = [ref[lead + (pl.ds(row0 * TOKEN_PITCH + cc, rows, stride=TOKEN_PITCH), slice(None))]
             for cc in range(TOKEN_ROWS)]
    return jnp.concatenate(parts, axis=-1)


def _zero_spare_rows(ref, rows, lead=()):
    ref[lead + (pl.ds(TOKEN_ROWS, rows, stride=TOKEN_PITCH), slice(None))] = jnp.zeros((rows, LANES), F32)


def _idx_slot(ismem, s):
    return ismem.at[pl.ds(pl.multiple_of(s * IDX_SLOT, IDX_SLOT), IDX_SLOT)]


def _gather_rows_step(i, n_steps, idx_hbm, src_hbm, ismem, buf, isem, rsem):
    n_rows = IDX_SLOT
    slot = i % 2
    nslot = 1 - slot

    def idx_copy(step, s):
        return pltpu.make_async_copy(idx_hbm.at[step], _idx_slot(ismem, s), isem.at[s])

    def row_copy(row0, r, s):
        return pltpu.make_async_copy(src_hbm.at[pl.ds(row0, TOKEN_ROWS)],
                                     buf.at[s, pl.ds(r * TOKEN_PITCH, TOKEN_ROWS)], rsem.at[s])

    def issue_rows(s):
        def body(r, carry):
            row_copy(ismem[s * IDX_SLOT + r], r, s).start()
            return carry
        lax.fori_loop(0, n_rows, body, 0, unroll=DMA_UNROLL)

    def wait_rows(s):
        def body(r, carry):
            row_copy(0, r, s).wait()
            return carry
        lax.fori_loop(0, n_rows, body, 0, unroll=8)

    @pl.when(i == 0)
    def _():
        idx_copy(0, 0).start()
        idx_copy(0, 0).wait()
        issue_rows(0)

        @pl.when(n_steps > 1)
        def _():
            idx_copy(1, 1).start()

    @pl.when(i + 1 < n_steps)
    def _():
        idx_copy(i + 1, nslot).wait()
        issue_rows(nslot)

    @pl.when(i + 2 < n_steps)
    def _():
        idx_copy(i + 2, slot).start()

    wait_rows(slot)


def _dispatch_kernel(pad_row_ref, pad_cnt_ref, nv_ref, pos_hbm, x1_ref, mod_ref, nffn_ref, xs_hbm,
                     ismem, hbuf, zero_ref, isem, csem, zsem):
    i = pl.program_id(0)
    n_steps = pl.num_programs(0)
    n_pairs = IDX_SLOT
    tb = n_pairs // TOP_K
    slot = i % 2

    def idx_copy(step, s):
        return pltpu.make_async_copy(pos_hbm.at[step], _idx_slot(ismem, s), isem.at[s])

    def tile_copy(r, row0, s):
        return pltpu.make_async_copy(hbuf.at[s, pl.ds(r * TOKEN_PITCH, TOKEN_PITCH)],
                                     xs_hbm.at[pl.ds(row0, TOKEN_PITCH)], csem.at[s])

    def zero_copy(row0):
        return pltpu.make_async_copy(zero_ref.at[pl.ds(0, TOKEN_PITCH)],
                                     xs_hbm.at[pl.ds(row0, TOKEN_PITCH)], zsem.at[0])

    def for_pad_tiles(fn):
        def per_expert(e, carry):
            def per_tile(j, c2):
                fn(pad_row_ref[e] + j * TOKEN_PITCH)
                return c2
            return lax.fori_loop(0, pad_cnt_ref[e], per_tile, carry)
        lax.fori_loop(0, N_EXPERTS, per_expert, 0)

    tile_rows = zero_ref.shape[0]
    n_tiles = xs_hbm.shape[0] // tile_rows

    def tail_copy(j):
        return pltpu.make_async_copy(
            zero_ref, xs_hbm.at[pl.ds(pl.multiple_of(j * tile_rows, SUBLANES), tile_rows)], zsem.at[1])

    def for_tail_tiles(fn):
        def body(j, carry):
            fn(j)
            return carry
        lax.fori_loop(nv_ref[0], n_tiles, body, 0)

    @pl.when(i == 0)
    def _():
        idx_copy(0, 0).start()
        zero_ref[...] = jnp.zeros(zero_ref.shape, F32)
        for sl in range(2):
            _zero_spare_rows(hbuf, tb, (sl,))
        for_pad_tiles(lambda row0: zero_copy(row0).start())
        for_tail_tiles(lambda j: tail_copy(j).start())

    idx_copy(i, slot).wait()

    @pl.when(i + 1 < n_steps)
    def _():
        idx_copy(i + 1, 1 - slot).start()

    _store_token_tiles(hbuf, 0, _ffn_input(x1_ref[...], nffn_ref[...], mod_ref), (slot,))

    def issue(r, carry):
        for kk in range(TOP_K):
            tile_copy(r, ismem[slot * IDX_SLOT + kk * tb + r], slot).start()
        return carry
    lax.fori_loop(0, tb, issue, 0, unroll=DMA_UNROLL // TOP_K)

    def drain(s):
        def body(r, carry):
            tile_copy(0, 0, s).wait()
            return carry
        lax.fori_loop(0, n_pairs, body, 0, unroll=8)

    @pl.when(i > 0)
    def _():
        drain(1 - slot)

    @pl.when(i == n_steps - 1)
    def _():
        drain(slot)
        for_pad_tiles(lambda row8: zero_copy(0).wait())
        for_tail_tiles(lambda j: tail_copy(0).wait())


def _dispatch(pad_row8, pad_cnt, n_valid, pos8_blocks, x1, mod3, nffn, n_rows, s):
    n_blocks, n_pairs = pos8_blocks.shape
    n, d = x1.shape
    tb = n_pairs // TOP_K
    tpb = s // tb
    grid_spec = pltpu.PrefetchScalarGridSpec(
        num_scalar_prefetch=3,
        grid=(n_blocks,),
        in_specs=[pl.BlockSpec(memory_space=pl.ANY),
                  pl.BlockSpec((tb, d), lambda i, pr, pc, nv: (i, 0)),
                  pl.BlockSpec((1, 6, d), lambda i, pr, pc, nv: (i // tpb, 0, 0)),
                  pl.BlockSpec((1, d), lambda i, pr, pc, nv: (0, 0))],
        out_specs=pl.BlockSpec(memory_space=pl.ANY),
        scratch_shapes=[pltpu.SMEM((2 * IDX_SLOT,), jnp.int32),
                        pltpu.VMEM((2, tb * TOKEN_PITCH, LANES), F32),
                        pltpu.VMEM((T_MOE * TOKEN_PITCH, LANES), F32),
                        pltpu.SemaphoreType.DMA((2,)),
                        pltpu.SemaphoreType.DMA((2,)),
                        pltpu.SemaphoreType.DMA((2,))])
    return pl.pallas_call(
        _dispatch_kernel,
        grid_spec=grid_spec,
        out_shape=jax.ShapeDtypeStruct((n_rows * TOKEN_PITCH, LANES), F32),
        compiler_params=_cparams(("arbitrary",)),
        name="dispatch",
    )(pad_row8, pad_cnt, n_valid, pos8_blocks, x1, mod3, nffn)


def _moe_kernel(te_ref, nv_ref, xs_ref, wg_f32, wu_f32, wd_f32, y_ref, wg_ref, wu_ref, wd_ref):
    i = pl.program_id(0)
    t = xs_ref.shape[0] // TOKEN_PITCH

    @pl.when((i == 0) | (te_ref[i] != te_ref[jnp.maximum(i - 1, 0)]))
    def _():
        wg_ref[...] = wg_f32[...].astype(BF16)
        wu_ref[...] = wu_f32[...].astype(BF16)
        wd_ref[...] = wd_f32[...].astype(BF16)

    @pl.when(i < nv_ref[0])
    def _():
        half = t // 2
        acts = []
        for hb in range(2):
            xg = _load_token_tiles(xs_ref, hb * half, half).astype(BF16)
            g = jnp.dot(xg, wg_ref[0], preferred_element_type=F32)
            u = jnp.dot(xg, wu_ref[0], preferred_element_type=F32)
            acts.append((g, u))
        for hb in range(2):
            g, u = acts[hb]
            a = (g * _sigmoid(g) * u).astype(BF16)
            _store_token_tiles(y_ref, hb * half, jnp.dot(a, wd_ref[0], preferred_element_type=F32))
        _zero_spare_rows(y_ref, t)

    @pl.when(i >= nv_ref[0])
    def _():
        y_ref[...] = jnp.zeros(y_ref.shape, F32)


def _moe(tile_expert, n_valid, xs, wg, wu, wd):
    t = T_MOE
    n_tiles = xs.shape[0] // (t * TOKEN_PITCH)
    d = wg.shape[1]
    de = wg.shape[2]
    grid_spec = pltpu.PrefetchScalarGridSpec(
        num_scalar_prefetch=2,
        grid=(n_tiles,),
        in_specs=[pl.BlockSpec((t * TOKEN_PITCH, LANES), lambda i, te, nv: (jnp.minimum(i, nv[0] - 1), 0)),
                  pl.BlockSpec((1, d, de), lambda i, te, nv: (te[i], 0, 0)),
                  pl.BlockSpec((1, d, de), lambda i, te, nv: (te[i], 0, 0)),
                  pl.BlockSpec((1, de, d), lambda i, te, nv: (te[i], 0, 0))],
        out_specs=pl.BlockSpec((t * TOKEN_PITCH, LANES), lambda i, te, nv: (i, 0)),
        scratch_shapes=[pltpu.VMEM((1, d, de), BF16), pltpu.VMEM((1, d, de), BF16),
                        pltpu.VMEM((1, de, d), BF16)])
    return pl.pallas_call(
        _moe_kernel,
        grid_spec=grid_spec,
        out_shape=jax.ShapeDtypeStruct(xs.shape, F32),
        compiler_params=_cparams(("arbitrary",)),
        name="moe",
    )(tile_expert, n_valid, xs, wg, wu, wd)


def _combine_kernel(pos_hbm, y_hbm, x1_ref, route_ref, mod_ref, o_ref, ismem, ybuf, isem, rsem):
    i = pl.program_id(0)
    tm = x1_ref.shape[0]
    _gather_rows_step(i, pl.num_programs(0), pos_hbm, y_hbm, ismem, ybuf, isem, rsem)
    slot = i % 2
    g_f = mod_ref[0, 5:6, :]
    w0 = route_ref[:, 0:1]
    w1 = route_ref[:, 1:2]
    y = (w0 * _load_token_tiles(ybuf, 0, tm, (slot,)) + w1 * _load_token_tiles(ybuf, tm, tm, (slot,)))
    o_ref[...] = x1_ref[...] + g_f * y


def _combine(pos2, y_sorted, x1, route, mod3, s):
    n, d = x1.shape
    tm = TM_COMB
    tpb = s // tm
    return pl.pallas_call(
        _combine_kernel,
        grid=(n // tm,),
        in_specs=[pl.BlockSpec(memory_space=pl.ANY),
                  pl.BlockSpec(memory_space=pl.ANY),
                  pl.BlockSpec((tm, d), lambda i: (i, 0)),
                  pl.BlockSpec((tm, LANES), lambda i: (i, 0)),
                  pl.BlockSpec((1, 6, d), lambda i: (i // tpb, 0, 0))],
        out_specs=pl.BlockSpec((tm, d), lambda i: (i, 0)),
        out_shape=jax.ShapeDtypeStruct((n, d), F32),
        scratch_shapes=[pltpu.SMEM((2 * IDX_SLOT,), jnp.int32),
                        pltpu.VMEM((2, TOP_K * tm * TOKEN_PITCH, LANES), F32),
                        pltpu.SemaphoreType.DMA((2,)),
                        pltpu.SemaphoreType.DMA((2,))],
        compiler_params=_cparams(("arbitrary",)),
        name="combine",
    )(pos2, y_sorted, x1, route, mod3)


def _routing_tables(route, n):
    t = T_MOE
    flat_e = route[:, 2:4].astype(jnp.int32).reshape(-1)
    onehot = (flat_e[:, None] == jnp.arange(N_EXPERTS, dtype=jnp.int32)[None, :]).astype(jnp.int32)
    csum = jnp.cumsum(onehot, axis=0)
    rank = jnp.sum(onehot * csum, axis=1) - 1
    counts = csum[-1]
    ptiles = (counts + t - 1) // t
    tile_end = jnp.cumsum(ptiles)
    tile_start = tile_end - ptiles
    pos = jnp.sum(onehot * (tile_start * t)[None, :], axis=1) + rank
    n_tiles = (TOP_K * n) // t + N_EXPERTS
    n_valid = tile_end[-1]
    tile_ids = jnp.arange(n_tiles, dtype=jnp.int32)
    te = jnp.sum((tile_ids[:, None] >= tile_end[None, :]).astype(jnp.int32), axis=1)
    te_last = jnp.sum((n_valid - 1 >= tile_end).astype(jnp.int32))
    tile_expert = jnp.minimum(jnp.where(tile_ids < n_valid, te, te_last), N_EXPERTS - 1)
    pos8 = (pos * TOKEN_PITCH).astype(jnp.int32).reshape(n, TOP_K)
    blocks = lambda tb: pos8.reshape(n // tb, tb, TOP_K).transpose(0, 2, 1).reshape(n // tb, TOP_K * tb)
    pad_row8 = ((tile_start * t + counts) * TOKEN_PITCH).astype(jnp.int32)
    pad_cnt = (ptiles * t - counts).astype(jnp.int32)
    return (tile_expert.astype(jnp.int32), n_valid.astype(jnp.int32).reshape(1), pad_row8, pad_cnt,
            blocks(TB_DISPATCH), blocks(TM_COMB), n_tiles * t)


def _pad_lanes(a, left, total):
    pad = [(0, 0)] * (a.ndim - 1) + [(left, total - left - a.shape[-1])]
    return jnp.pad(a, pad)


def _layer(x2, c, cos_t, sin_t, b, s, w_ada, b_ada, norm_mix, w_in, conv_w, conv_b, conv_ln_g, conv_ln_b,
           q_a_norm, w_q_b, kv_a_norm, w_kv_b, q_norm, k_norm, w_out, norm_ffn, w_group, b_group,
           w_expert, b_expert, w_gate_e, w_up_e, w_down_e):
    n, d = x2.shape
    c_conv = conv_w.shape[1]
    q_rank = q_a_norm.shape[0]
    kv_rank = kv_a_norm.shape[0]
    row = lambda a: a.reshape(1, -1)

    o1 = 2 * c_conv
    o2 = o1 + q_rank
    o3 = o2 + kv_rank
    win_p = jnp.concatenate([w_in[:, :o3], _pad_lanes(w_in[:, o3:], QK_NOPE_DIM, LANES)], axis=1).astype(BF16)
    wq_p = _pad_lanes(w_q_b.reshape(q_rank, N_HEADS, QK_HEAD_DIM), 0, LANES).reshape(q_rank, N_HEADS * LANES)
    wkv = w_kv_b.reshape(kv_rank, N_HEADS, QK_NOPE_DIM + V_HEAD_DIM)
    wk_p = _pad_lanes(wkv[..., :QK_NOPE_DIM], 0, LANES).reshape(kv_rank, N_HEADS * LANES)
    wv = wkv[..., QK_NOPE_DIM:].reshape(kv_rank, N_HEADS * V_HEAD_DIM)
    gain_t = lambda g: jnp.broadcast_to(_pad_lanes(row(g), 0, LANES).reshape(LANES, 1), (LANES, TM_PROJ))
    qn_t = gain_t(q_norm)
    kn_t = gain_t(k_norm)
    wr = _pad_lanes(jnp.concatenate([w_expert, w_group], axis=1), 0, LANES).astype(BF16)
    br = _pad_lanes(row(jnp.concatenate([b_expert, b_group])), 0, LANES)
    conv_w_p = jnp.pad(conv_w, ((0, CONV_HALO - CONV_K), (0, 0)))

    mod3 = _ada(c, w_ada, b_ada).reshape(b, 6, d)
    u, qt4, k4, vt3 = _inproj(x2, mod3, row(norm_mix), win_p, row(q_a_norm), wq_p.T.astype(BF16),
                              row(kv_a_norm), wk_p.T.astype(BF16), wv.T.astype(BF16), qn_t, kn_t,
                              cos_t, sin_t, b, s)
    y_conv = _conv(u.reshape(b, s, c_conv), conv_w_p, row(conv_b), row(conv_ln_g), row(conv_ln_b))
    y_attn = _attn(qt4, k4, vt3)
    x1, route = _outproj(y_conv.reshape(n, c_conv), y_attn.reshape(n, -1), x2, mod3,
                         w_out.astype(BF16), row(norm_ffn), wr, br, s)
    tile_expert, n_valid, pad_row8, pad_cnt, pos8_disp, pos8_comb, n_rows = _routing_tables(route, n)
    xs = _dispatch(pad_row8, pad_cnt, n_valid, pos8_disp, x1, mod3, row(norm_ffn), n_rows, s)
    ys = _moe(tile_expert, n_valid, xs, w_gate_e, w_up_e, w_down_e)
    return _combine(pos8_comb, ys, x1, route, mod3, s)


def kernel(x, c, positions, w_ada, b_ada, norm_mix, w_in, conv_w, conv_b, conv_ln_g, conv_ln_b, q_a_norm,
           w_q_b, kv_a_norm, w_kv_b, q_norm, k_norm, w_out, norm_ffn, w_group, b_group, w_expert, b_expert,
           w_gate_e, w_up_e, w_down_e):
    b, s, d = x.shape
    n = b * s
    inv_freq = ROPE_THETA ** (-jnp.arange(0, QK_ROPE_DIM, 2, dtype=F32) / QK_ROPE_DIM)
    ang = inv_freq[:, None] * positions.astype(F32).reshape(1, n)
    cos_t, sin_t = jnp.cos(ang), jnp.sin(ang)
    x2 = x.reshape(n, d)
    for l in range(w_ada.shape[0]):
        x2 = _layer(x2, c, cos_t, sin_t, b, s, w_ada[l], b_ada[l], norm_mix[l], w_in[l], conv_w[l], conv_b[l],
                    conv_ln_g[l], conv_ln_b[l], q_a_norm[l], w_q_b[l], kv_a_norm[l], w_kv_b[l], q_norm[l],
                    k_norm[l], w_out[l], norm_ffn[l], w_group[l], b_group[l], w_expert[l], b_expert[l],
                    w_gate_e[l], w_up_e[l], w_down_e[l])
    return x2.reshape(b, s, d)
```

```python
import functools

import jax
import jax.numpy as jnp
from jax import lax
from jax.experimental import pallas as pl
from jax.experimental.pallas import tpu as pltpu

F32 = jnp.float32
BF16 = jnp.bfloat16

CHUNK = 64
CONV_K = 31
N_HEADS = 8
V_HEAD_DIM = 64
QK_NOPE_DIM = 64
QK_ROPE_DIM = 32
QK_HEAD_DIM = QK_NOPE_DIM + QK_ROPE_DIM
N_GROUPS = 4
EXPERTS_PER_GROUP = 8
N_EXPERTS = N_GROUPS * EXPERTS_PER_GROUP
TOP_K = 2
ROPE_THETA = 10000.0
EPS = 1e-6
LOG2_E = 1.4426950408889634

LANES = 128
SUBLANES = 8
TOKEN_ROWS = 8
TOKEN_PITCH = TOKEN_ROWS + 1
CONV_HALO = 32

TM_PROJ = 512
TS_CONV = 256
CONV_ROWS = 32
TQ_ATTN = 512
DENOM_ROWS = 16
HEADS_PER_STEP = 4
T_MOE = 256
IDX_SLOT = 1024
TB_DISPATCH = IDX_SLOT // TOP_K
TM_COMB = IDX_SLOT // TOP_K
VMEM_LIMIT = 48 * 1024 * 1024


def _sigmoid(v):
    return 1.0 / (1.0 + jnp.exp(-v))


def _cparams(sem):
    return pltpu.CompilerParams(dimension_semantics=sem, vmem_limit_bytes=VMEM_LIMIT)


def _ada_kernel(c_ref, w_ref, b_ref, o_ref):
    c = c_ref[...]
    s = c * _sigmoid(c)
    o_ref[...] = jnp.dot(s, w_ref[...], preferred_element_type=F32,
                         precision=lax.Precision.HIGHEST) + b_ref[...]


def _ada(c, w_ada, b_ada):
    b, d = c.shape
    n_out = w_ada.shape[1]
    return pl.pallas_call(
        _ada_kernel,
        grid=(n_out // d,),
        in_specs=[pl.BlockSpec((b, d), lambda j: (0, 0)),
                  pl.BlockSpec((d, d), lambda j: (0, j)),
                  pl.BlockSpec((1, d), lambda j: (0, j))],
        out_specs=pl.BlockSpec((b, d), lambda j: (0, j)),
        out_shape=jax.ShapeDtypeStruct((b, n_out), F32),
        compiler_params=_cparams(("arbitrary",)),
        name="ada",
    )(c, w_ada, b_ada.reshape(1, n_out))


def _head_norm_rope_t(tt, gain, cos_t, sin_t):
    live = tt[:QK_HEAD_DIM]
    r = lax.rsqrt(jnp.sum(live * live, axis=0, keepdims=True) * (1.0 / QK_HEAD_DIM) + EPS)
    tn = live * r * gain[:QK_HEAD_DIM]
    half = QK_ROPE_DIM // 2
    lo = tn[QK_NOPE_DIM:QK_NOPE_DIM + half]
    hi = tn[QK_NOPE_DIM + half:QK_HEAD_DIM]
    return jnp.concatenate([tn[:QK_NOPE_DIM], lo * cos_t - hi * sin_t, hi * cos_t + lo * sin_t,
                            jnp.zeros((LANES - QK_HEAD_DIM, tt.shape[1]), F32)], axis=0)


def _inproj_kernel(x_ref, mod_ref, nmix_ref, win_ref, qan_ref, wqt_ref, kvan_ref, wkt_ref, wvt_ref,
                   qn_ref, kn_ref, cos_ref, sin_ref, u_ref, qt_ref, k_ref, vt_ref):
    x = x_ref[...]
    sh_a = mod_ref[0, 0:1, :]
    sc_a = mod_ref[0, 1:2, :]
    y = x * lax.rsqrt(jnp.mean(x * x, axis=-1, keepdims=True) + EPS) * nmix_ref[...]
    h = y * (1.0 + sc_a) + sh_a
    proj = jnp.dot(h.astype(BF16), win_ref[...], preferred_element_type=F32)
    c_conv = u_ref.shape[-1]
    val = proj[:, :c_conv]
    gate = proj[:, c_conv:2 * c_conv]
    u_ref[...] = (val * _sigmoid(gate)).astype(BF16)
    o = 2 * c_conv
    q_rank = wqt_ref.shape[1]
    kv_rank = wkt_ref.shape[1]
    cq = proj[:, o:o + q_rank]
    ckv = proj[:, o + q_rank:o + q_rank + kv_rank]
    kr = proj[:, o + q_rank + kv_rank:o + q_rank + kv_rank + LANES]
    cqn = cq * lax.rsqrt(jnp.mean(cq * cq, axis=-1, keepdims=True) + EPS) * qan_ref[...]
    ckvn = ckv * lax.rsqrt(jnp.mean(ckv * ckv, axis=-1, keepdims=True) + EPS) * kvan_ref[...]
    cqn_t = cqn.T.astype(BF16)
    ckvn_t = ckvn.T.astype(BF16)
    kr_t = kr.T
    qt_all = jnp.dot(wqt_ref[...], cqn_t, preferred_element_type=F32)
    kt_all = jnp.dot(wkt_ref[...], ckvn_t, preferred_element_type=F32)
    vt_ref[0] = jnp.dot(wvt_ref[...], ckvn_t, preferred_element_type=F32).astype(BF16)
    cos_t = cos_ref[...]
    sin_t = sin_ref[...]
    scale = QK_HEAD_DIM ** -0.5 * LOG2_E
    for hd in range(N_HEADS):
        qh = qt_all[hd * LANES:(hd + 1) * LANES]
        qt_ref[0, hd] = (_head_norm_rope_t(qh, qn_ref[...], cos_t, sin_t) * scale).astype(BF16)
        kh = kt_all[hd * LANES:(hd + 1) * LANES] + kr_t
        k_ref[0, hd] = _head_norm_rope_t(kh, kn_ref[...], cos_t, sin_t).T.astype(BF16)


def _inproj(x2, mod3, nmix, win_p, qan, wqt_p, kvan, wkt_p, wvt, qn_t, kn_t, cos_t, sin_t, b, s):
    n, d = x2.shape
    tm = TM_PROJ
    tpb = s // tm
    c_conv = (win_p.shape[1] - wqt_p.shape[1] - wkt_p.shape[1] - LANES) // 2
    half = QK_ROPE_DIM // 2
    full = lambda a: pl.BlockSpec(a.shape, lambda i: (0,) * a.ndim)
    return pl.pallas_call(
        _inproj_kernel,
        grid=(n // tm,),
        in_specs=[pl.BlockSpec((tm, d), lambda i: (i, 0)),
                  pl.BlockSpec((1, 6, d), lambda i: (i // tpb, 0, 0)),
                  full(nmix), full(win_p), full(qan), full(wqt_p), full(kvan), full(wkt_p), full(wvt),
                  full(qn_t), full(kn_t),
                  pl.BlockSpec((half, tm), lambda i: (0, i)),
                  pl.BlockSpec((half, tm), lambda i: (0, i))],
        out_specs=[pl.BlockSpec((tm, c_conv), lambda i: (i, 0)),
                   pl.BlockSpec((1, N_HEADS, LANES, tm), lambda i: (i // tpb, 0, 0, i % tpb)),
                   pl.BlockSpec((1, N_HEADS, tm, LANES), lambda i: (i // tpb, 0, i % tpb, 0)),
                   pl.BlockSpec((1, c_conv, tm), lambda i: (i // tpb, 0, i % tpb))],
        out_shape=[jax.ShapeDtypeStruct((n, c_conv), BF16),
                   jax.ShapeDtypeStruct((b, N_HEADS, LANES, s), BF16),
                   jax.ShapeDtypeStruct((b, N_HEADS, s, LANES), BF16),
                   jax.ShapeDtypeStruct((b, N_HEADS * V_HEAD_DIM, s), BF16)],
        compiler_params=_cparams(("arbitrary",)),
        name="inproj",
    )(x2, mod3, nmix, win_p, qan, wqt_p, kvan, wkt_p, wvt, qn_t, kn_t, cos_t, sin_t)


def _conv_kernel(prev_ref, cur_ref, w_ref, b_ref, g_ref, beta_ref, o_ref, buf_ref, shift_ref):
    j = pl.program_id(1)
    ts = cur_ref.shape[1]

    @pl.when(j == 0)
    def _():
        buf_ref[0:CONV_HALO, :] = jnp.zeros((CONV_HALO, buf_ref.shape[1]), F32)

    @pl.when(j > 0)
    def _():
        buf_ref[0:CONV_HALO, :] = prev_ref[0].astype(F32)

    buf_ref[CONV_HALO:CONV_HALO + ts, :] = cur_ref[0].astype(F32)
    span = CONV_HALO + ts - SUBLANES
    for sh in range(1, SUBLANES):
        shift_ref[sh - 1, 0:span, :] = buf_ref[sh:sh + span, :]
    first = CONV_HALO - (CONV_K - 1)
    for r0 in range(0, ts, CONV_ROWS):
        acc = None
        for kk in range(CONV_K):
            sh = (first + kk) % SUBLANES
            base = r0 + (first + kk) - sh
            if sh == 0:
                win = buf_ref[base:base + CONV_ROWS, :]
            else:
                win = shift_ref[sh - 1, base:base + CONV_ROWS, :]
            term = win * w_ref[kk:kk + 1, :]
            acc = term if acc is None else acc + term
        dw = acc + b_ref[...]
        mu = jnp.mean(dw, axis=-1, keepdims=True)
        cen = dw - mu
        var = jnp.mean(cen * cen, axis=-1, keepdims=True)
        z = cen * lax.rsqrt(var + EPS) * g_ref[...] + beta_ref[...]
        o_ref[0, r0:r0 + CONV_ROWS, :] = (z * _sigmoid(z)).astype(BF16)


def _conv(u3, conv_w_p, conv_b, ln_g, ln_b):
    b, s, c = u3.shape
    ts = TS_CONV
    hb = ts // CONV_HALO
    return pl.pallas_call(
        _conv_kernel,
        grid=(b, s // ts),
        in_specs=[pl.BlockSpec((1, CONV_HALO, c), lambda bi, j: (bi, jnp.maximum(j * hb - 1, 0), 0)),
                  pl.BlockSpec((1, ts, c), lambda bi, j: (bi, j, 0)),
                  pl.BlockSpec(conv_w_p.shape, lambda bi, j: (0, 0)),
                  pl.BlockSpec((1, c), lambda bi, j: (0, 0)),
                  pl.BlockSpec((1, c), lambda bi, j: (0, 0)),
                  pl.BlockSpec((1, c), lambda bi, j: (0, 0))],
        out_specs=pl.BlockSpec((1, ts, c), lambda bi, j: (bi, j, 0)),
        out_shape=jax.ShapeDtypeStruct((b, s, c), BF16),
        scratch_shapes=[pltpu.VMEM((CONV_HALO + ts, c), F32),
                        pltpu.VMEM((SUBLANES - 1, CONV_HALO + ts - SUBLANES, c), F32)],
        compiler_params=_cparams(("arbitrary", "arbitrary")),
        name="conv",
    )(u3, u3, conv_w_p, conv_b, ln_g, ln_b)


def _attn_kernel(qt_ref, k_ref, vt_ref, o_ref, sa_ref, sb_ref):
    qi = pl.program_id(2)
    tq = qt_ref.shape[3]
    heads = qt_ref.shape[1]
    key_chunk = lax.broadcasted_iota(jnp.int32, (tq, tq), 0) // CHUNK
    qry_chunk = lax.broadcasted_iota(jnp.int32, (tq, tq), 1) // CHUNK
    diag_mask = key_chunk <= qry_chunk
    ones_rows = (lax.broadcasted_iota(jnp.int32, (DENOM_ROWS, tq), 0) == 0).astype(BF16)

    def scores(jt, s_ref):
        start = pl.multiple_of(jt * tq, tq)
        for hh in range(heads):
            kj = k_ref[0, hh, pl.ds(start, tq), :]
            s_ref[hh] = jnp.dot(kj, qt_ref[0, hh], preferred_element_type=F32)

    def consume(jt, s_ref, carry, masked):
        start = pl.multiple_of(jt * tq, tq)
        new = []
        for hh in range(heads):
            m, acc = carry[hh]
            st = s_ref[hh]
            if masked:
                st = jnp.where(diag_mask, st, -jnp.inf)
            vj = vt_ref[0, hh * V_HEAD_DIM:(hh + 1) * V_HEAD_DIM, pl.ds(start, tq)]
            m_new = jnp.maximum(m, jnp.max(st, axis=0, keepdims=True))
            alpha = jnp.exp2(m - m_new)
            p = jnp.exp2(st - m_new)
            v_ext = jnp.concatenate([vj, ones_rows], axis=0)
            acc_new = alpha * acc + jnp.dot(v_ext, p.astype(BF16), preferred_element_type=F32)
            new.append((m_new, acc_new))
        return tuple(new)

    def body(u, carry):
        scores(2 * u + 1, sb_ref)
        carry = consume(2 * u, sa_ref, carry, False)
        scores(2 * u + 2, sa_ref)
        carry = consume(2 * u + 1, sb_ref, carry, False)
        return carry

    init = tuple((jnp.full((1, tq), -jnp.inf, F32), jnp.zeros((V_HEAD_DIM + DENOM_ROWS, tq), F32))
                 for _ in range(heads))
    scores(0, sa_ref)
    carry = lax.fori_loop(0, qi // 2, body, init)

    def last_even(carry):
        return consume(qi, sa_ref, carry, True)

    def last_odd(carry):
        scores(qi, sb_ref)
        carry = consume(qi - 1, sa_ref, carry, False)
        return consume(qi, sb_ref, carry, True)

    carry = lax.cond(qi % 2 == 1, last_odd, last_even, carry)
    out_t = jnp.concatenate([acc[:V_HEAD_DIM] / acc[V_HEAD_DIM:V_HEAD_DIM + 1] for (_, acc) in carry],
                            axis=0)
    o_ref[0] = out_t.T.astype(BF16)


def _attn(qt4, k4, vt3):
    b, nh, s, _ = k4.shape
    tq = TQ_ATTN
    hps = HEADS_PER_STEP
    return pl.pallas_call(
        _attn_kernel,
        scratch_shapes=[pltpu.VMEM((hps, tq, tq), F32), pltpu.VMEM((hps, tq, tq), F32)],
        grid=(b, nh // hps, s // tq),
        in_specs=[pl.BlockSpec((1, hps, LANES, tq), lambda bi, hp, qi: (bi, hp, 0, qi)),
                  pl.BlockSpec((1, hps, s, LANES), lambda bi, hp, qi: (bi, hp, 0, 0)),
                  pl.BlockSpec((1, hps * V_HEAD_DIM, s), lambda bi, hp, qi: (bi, hp, 0))],
        out_specs=pl.BlockSpec((1, tq, hps * V_HEAD_DIM), lambda bi, hp, qi: (bi, qi, hp)),
        out_shape=jax.ShapeDtypeStruct((b, s, nh * V_HEAD_DIM), BF16),
        compiler_params=_cparams(("arbitrary", "arbitrary", "arbitrary")),
        name="attn",
    )(qt4, k4, vt3)


def _ffn_input(x1, nffn, mod_ref):
    sh_f = mod_ref[0, 3:4, :]
    sc_f = mod_ref[0, 4:5, :]
    y = x1 * lax.rsqrt(jnp.mean(x1 * x1, axis=-1, keepdims=True) + EPS) * nffn
    return y * (1.0 + sc_f) + sh_f


def _outproj_kernel(yc_ref, ya_ref, x_ref, mod_ref, wout_ref, nffn_ref, wr_ref, br_ref,
                    x1_ref, route_ref):
    tm = x_ref.shape[0]
    half = tm // 2
    mixed = [jnp.dot(jnp.concatenate([yc_ref[r0:r0 + half, :], ya_ref[r0:r0 + half, :]], axis=-1),
                     wout_ref[...], preferred_element_type=F32) for r0 in (0, half)]
    for hb, r0 in enumerate((0, half)):
        _outproj_rows(r0, half, mixed[hb], x_ref, mod_ref, nffn_ref, wr_ref, br_ref, x1_ref, route_ref)


def _outproj_rows(r0, rows, mixed, x_ref, mod_ref, nffn_ref, wr_ref, br_ref, x1_ref, route_ref):
    g_a = mod_ref[0, 2:3, :]
    x1 = x_ref[r0:r0 + rows, :] + g_a * mixed
    x1_ref[r0:r0 + rows, :] = x1
    h2 = _ffn_input(x1, nffn_ref[...], mod_ref)
    logits = jnp.dot(h2.astype(BF16), wr_ref[...], preferred_element_type=F32) + br_ref[...]
    lane = lax.broadcasted_iota(jnp.int32, logits.shape, 1)
    lane_f = lane.astype(F32)
    big = float(LANES)
    is_g = (lane >= N_EXPERTS) & (lane < N_EXPERTS + N_GROUPS)
    gl = jnp.where(is_g, logits, -jnp.inf)
    gmax = jnp.max(gl, axis=-1, keepdims=True)
    g_top = 1.0 / jnp.sum(jnp.exp(gl - gmax), axis=-1, keepdims=True)
    g_idx = jnp.min(jnp.where(gl == gmax, lane_f - N_EXPERTS, big), axis=-1, keepdims=True)
    in_grp = (lane < N_EXPERTS) & ((lane // EXPERTS_PER_GROUP).astype(F32) == g_idx)
    el = jnp.where(in_grp, logits, -jnp.inf)
    e1 = jnp.max(el, axis=-1, keepdims=True)
    i1 = jnp.min(jnp.where(el == e1, lane_f, big), axis=-1, keepdims=True)
    el2 = jnp.where(lane_f == i1, -jnp.inf, el)
    e2 = jnp.max(el2, axis=-1, keepdims=True)
    i2 = jnp.min(jnp.where(el2 == e2, lane_f, big), axis=-1, keepdims=True)
    t = jnp.exp(e2 - e1)
    w1 = g_top / (1.0 + t)
    w2 = g_top * t / (1.0 + t)
    route_ref[r0:r0 + rows, :] = jnp.where(lane == 0, w1, jnp.where(lane == 1, w2, jnp.where(
        lane == 2, i1, jnp.where(lane == 3, i2, 0.0))))


def _outproj(yc, ya, x2, mod3, wout, nffn, wr, br, s):
    n, d = x2.shape
    tm = TM_PROJ
    tpb = s // tm
    c = yc.shape[1]
    full = lambda a: pl.BlockSpec(a.shape, lambda i: (0,) * a.ndim)
    return pl.pallas_call(
        _outproj_kernel,
        grid=(n // tm,),
        in_specs=[pl.BlockSpec((tm, c), lambda i: (i, 0)),
                  pl.BlockSpec((tm, c), lambda i: (i, 0)),
                  pl.BlockSpec((tm, d), lambda i: (i, 0)),
                  pl.BlockSpec((1, 6, d), lambda i: (i // tpb, 0, 0)),
                  full(wout), full(nffn), full(wr), full(br)],
        out_specs=[pl.BlockSpec((tm, d), lambda i: (i, 0)),
                   pl.BlockSpec((tm, LANES), lambda i: (i, 0))],
        out_shape=[jax.ShapeDtypeStruct((n, d), F32),
                   jax.ShapeDtypeStruct((n, LANES), F32)],
        compiler_params=_cparams(("arbitrary",)),
        name="outproj",
    )(yc, ya, x2, mod3, wout, nffn, wr, br)


def _store_token_tiles(ref, row0, val, lead=()):
    rows = val.shape[0]
    for cc in range(TOKEN_ROWS):
        ref[lead + (pl.ds(row0 * TOKEN_PITCH + cc, rows, stride=TOKEN_PITCH), slice(None))] = (
            val[:, cc * LANES:(cc + 1) * LANES])


def _load_token_tiles(ref, row0, rows, lead=()):
    parts = [ref[lead + (pl.ds(row0 * TOKEN_PITCH + cc, rows, stride=TOKEN_PITCH), slice(None))]
             for cc in range(TOKEN_ROWS)]
    return jnp.concatenate(parts, axis=-1)


def _zero_spare_rows(ref, rows, lead=()):
    ref[lead + (pl.ds(TOKEN_ROWS, rows, stride=TOKEN_PITCH), slice(None))] = jnp.zeros((rows, LANES), F32)


def _idx_slot(ismem, s):
    return ismem.at[pl.ds(pl.multiple_of(s * IDX_SLOT, IDX_SLOT), IDX_SLOT)]


def _gather_rows_step(i, n_steps, idx_hbm, src_hbm, ismem, buf, isem, rsem):
    n_rows = IDX_SLOT
    slot = i % 2
    nslot = 1 - slot

    def idx_copy(step, s):
        return pltpu.make_async_copy(idx_hbm.at[step], _idx_slot(ismem, s), isem.at[s])

    def row_copy(row0, r, s):
        return pltpu.make_async_copy(src_hbm.at[pl.ds(row0, TOKEN_ROWS)],
                                     buf.at[s, pl.ds(r * TOKEN_PITCH, TOKEN_ROWS)], rsem.at[s])

    def issue_rows(s):
        def body(j, carry):
            for u in range(2):
                r = 2 * j + u
                row_copy(ismem[s * IDX_SLOT + r], r, s).start(priority=u)
            return carry
        lax.fori_loop(0, n_rows // 2, body, 0, unroll=4)

    def wait_rows(s):
        def body(r, carry):
            row_copy(0, r, s).wait()
            return carry
        lax.fori_loop(0, n_rows, body, 0, unroll=8)

    @pl.when(i == 0)
    def _():
        idx_copy(0, 0).start()
        idx_copy(0, 0).wait()
        issue_rows(0)

        @pl.when(n_steps > 1)
        def _():
            idx_copy(1, 1).start()

    @pl.when(i + 1 < n_steps)
    def _():
        idx_copy(i + 1, nslot).wait()
        issue_rows(nslot)

    @pl.when(i + 2 < n_steps)
    def _():
        idx_copy(i + 2, slot).start()

    wait_rows(slot)


def _dispatch_kernel(pad_row_ref, pad_cnt_ref, nv_ref, pos_hbm, x1_ref, mod_ref, nffn_ref, xs_hbm,
                     ismem, hbuf, zero_ref, isem, csem, zsem):
    i = pl.program_id(0)
    n_steps = pl.num_programs(0)
    n_pairs = IDX_SLOT
    tb = n_pairs // TOP_K
    slot = i % 2

    def idx_copy(step, s):
        return pltpu.make_async_copy(pos_hbm.at[step], _idx_slot(ismem, s), isem.at[s])

    def tile_copy(r, row0, s):
        return pltpu.make_async_copy(hbuf.at[s, pl.ds(r * TOKEN_PITCH, TOKEN_PITCH)],
                                     xs_hbm.at[pl.ds(row0, TOKEN_PITCH)], csem.at[s])

    def zero_copy(row0):
        return pltpu.make_async_copy(zero_ref.at[pl.ds(0, TOKEN_PITCH)],
                                     xs_hbm.at[pl.ds(row0, TOKEN_PITCH)], zsem.at[0])

    def for_pad_tiles(fn):
        def per_expert(e, carry):
            def per_tile(j, c2):
                fn(pad_row_ref[e] + j * TOKEN_PITCH)
                return c2
            return lax.fori_loop(0, pad_cnt_ref[e], per_tile, carry)
        lax.fori_loop(0, N_EXPERTS, per_expert, 0)

    tile_rows = zero_ref.shape[0]
    n_tiles = xs_hbm.shape[0] // tile_rows

    def tail_copy(j):
        return pltpu.make_async_copy(
            zero_ref, xs_hbm.at[pl.ds(pl.multiple_of(j * tile_rows, SUBLANES), tile_rows)], zsem.at[1])

    def for_tail_tiles(fn):
        def body(j, carry):
            fn(j)
            return carry
        lax.fori_loop(nv_ref[0], n_tiles, body, 0)

    @pl.when(i == 0)
    def _():
        idx_copy(0, 0).start()
        zero_ref[...] = jnp.zeros(zero_ref.shape, F32)
        for sl in range(2):
            _zero_spare_rows(hbuf, tb, (sl,))
        for_pad_tiles(lambda row0: zero_copy(row0).start())
        for_tail_tiles(lambda j: tail_copy(j).start())

    idx_copy(i, slot).wait()

    @pl.when(i + 1 < n_steps)
    def _():
        idx_copy(i + 1, 1 - slot).start()

    _store_token_tiles(hbuf, 0, _ffn_input(x1_ref[...], nffn_ref[...], mod_ref), (slot,))

    def issue(r, carry):
        for kk in range(TOP_K):
            tile_copy(r, ismem[slot * IDX_SLOT + kk * tb + r], slot).start(priority=kk)
        return carry
    lax.fori_loop(0, tb, issue, 0, unroll=4)

    def drain(s):
        def body(r, carry):
            tile_copy(0, 0, s).wait()
            return carry
        lax.fori_loop(0, n_pairs, body, 0, unroll=8)

    @pl.when(i > 0)
    def _():
        drain(1 - slot)

    @pl.when(i == n_steps - 1)
    def _():
        drain(slot)
        for_pad_tiles(lambda row8: zero_copy(0).wait())
        for_tail_tiles(lambda j: tail_copy(0).wait())


def _dispatch(pad_row8, pad_cnt, n_valid, pos8_blocks, x1, mod3, nffn, n_rows, s):
    n_blocks, n_pairs = pos8_blocks.shape
    n, d = x1.shape
    tb = n_pairs // TOP_K
    tpb = s // tb
    grid_spec = pltpu.PrefetchScalarGridSpec(
        num_scalar_prefetch=3,
        grid=(n_blocks,),
        in_specs=[pl.BlockSpec(memory_space=pl.ANY),
                  pl.BlockSpec((tb, d), lambda i, pr, pc, nv: (i, 0)),
                  pl.BlockSpec((1, 6, d), lambda i, pr, pc, nv: (i // tpb, 0, 0)),
                  pl.BlockSpec((1, d), lambda i, pr, pc, nv: (0, 0))],
        out_specs=pl.BlockSpec(memory_space=pl.ANY),
        scratch_shapes=[pltpu.SMEM((2 * IDX_SLOT,), jnp.int32),
                        pltpu.VMEM((2, tb * TOKEN_PITCH, LANES), F32),
                        pltpu.VMEM((T_MOE * TOKEN_PITCH, LANES), F32),
                        pltpu.SemaphoreType.DMA((2,)),
                        pltpu.SemaphoreType.DMA((2,)),
                        pltpu.SemaphoreType.DMA((2,))])
    return pl.pallas_call(
        _dispatch_kernel,
        grid_spec=grid_spec,
        out_shape=jax.ShapeDtypeStruct((n_rows * TOKEN_PITCH, LANES), F32),
        compiler_params=_cparams(("arbitrary",)),
        name="dispatch",
    )(pad_row8, pad_cnt, n_valid, pos8_blocks, x1, mod3, nffn)


def _moe_kernel(te_ref, nv_ref, xs_ref, wg_f32, wu_f32, wd_f32, y_ref, wg_ref, wu_ref, wd_ref):
    i = pl.program_id(0)
    t = xs_ref.shape[0] // TOKEN_PITCH

    @pl.when((i == 0) | (te_ref[i] != te_ref[jnp.maximum(i - 1, 0)]))
    def _():
        wg_ref[...] = wg_f32[...].astype(BF16)
        wu_ref[...] = wu_f32[...].astype(BF16)
        wd_ref[...] = wd_f32[...].astype(BF16)

    @pl.when(i < nv_ref[0])
    def _():
        half = t // 2
        acts = []
        for hb in range(2):
            xg = _load_token_tiles(xs_ref, hb * half, half).astype(BF16)
            g = jnp.dot(xg, wg_ref[0], preferred_element_type=F32)
            u = jnp.dot(xg, wu_ref[0], preferred_element_type=F32)
            acts.append((g, u))
        for hb in range(2):
            g, u = acts[hb]
            a = (g * _sigmoid(g) * u).astype(BF16)
            _store_token_tiles(y_ref, hb * half, jnp.dot(a, wd_ref[0], preferred_element_type=F32))
        _zero_spare_rows(y_ref, t)

    @pl.when(i >= nv_ref[0])
    def _():
        y_ref[...] = jnp.zeros(y_ref.shape, F32)


def _moe(tile_expert, n_valid, xs, wg, wu, wd):
    t = T_MOE
    n_tiles = xs.shape[0] // (t * TOKEN_PITCH)
    d = wg.shape[1]
    de = wg.shape[2]
    grid_spec = pltpu.PrefetchScalarGridSpec(
        num_scalar_prefetch=2,
        grid=(n_tiles,),
        in_specs=[pl.BlockSpec((t * TOKEN_PITCH, LANES), lambda i, te, nv: (jnp.minimum(i, nv[0] - 1), 0)),
                  pl.BlockSpec((1, d, de), lambda i, te, nv: (te[i], 0, 0)),
                  pl.BlockSpec((1, d, de), lambda i, te, nv: (te[i], 0, 0)),
                  pl.BlockSpec((1, de, d), lambda i, te, nv: (te[i], 0, 0))],
        out_specs=pl.BlockSpec((t * TOKEN_PITCH, LANES), lambda i, te, nv: (i, 0)),
        scratch_shapes=[pltpu.VMEM((1, d, de), BF16), pltpu.VMEM((1, d, de), BF16),
                        pltpu.VMEM((1, de, d), BF16)])
    return pl.pallas_call(
        _moe_kernel,
        grid_spec=grid_spec,
        out_shape=jax.ShapeDtypeStruct(xs.shape, F32),
        compiler_params=_cparams(("arbitrary",)),
        name="moe",
    )(tile_expert, n_valid, xs, wg, wu, wd)


def _combine_kernel(pos_hbm, y_hbm, x1_ref, route_ref, mod_ref, o_ref, ismem, ybuf, isem, rsem):
    i = pl.program_id(0)
    tm = x1_ref.shape[0]
    _gather_rows_step(i, pl.num_programs(0), pos_hbm, y_hbm, ismem, ybuf, isem, rsem)
    slot = i % 2
    g_f = mod_ref[0, 5:6, :]
    w0 = route_ref[:, 0:1]
    w1 = route_ref[:, 1:2]
    y = (w0 * _load_token_tiles(ybuf, 0, tm, (slot,)) + w1 * _load_token_tiles(ybuf, tm, tm, (slot,)))
    o_ref[...] = x1_ref[...] + g_f * y


def _combine(pos2, y_sorted, x1, route, mod3, s):
    n, d = x1.shape
    tm = TM_COMB
    tpb = s // tm
    return pl.pallas_call(
        _combine_kernel,
        grid=(n // tm,),
        in_specs=[pl.BlockSpec(memory_space=pl.ANY),
                  pl.BlockSpec(memory_space=pl.ANY),
                  pl.BlockSpec((tm, d), lambda i: (i, 0)),
                  pl.BlockSpec((tm, LANES), lambda i: (i, 0)),
                  pl.BlockSpec((1, 6, d), lambda i: (i // tpb, 0, 0))],
        out_specs=pl.BlockSpec((tm, d), lambda i: (i, 0)),
        out_shape=jax.ShapeDtypeStruct((n, d), F32),
        scratch_shapes=[pltpu.SMEM((2 * IDX_SLOT,), jnp.int32),
                        pltpu.VMEM((2, TOP_K * tm * TOKEN_PITCH, LANES), F32),
                        pltpu.SemaphoreType.DMA((2,)),
                        pltpu.SemaphoreType.DMA((2,))],
        compiler_params=_cparams(("arbitrary",)),
        name="combine",
    )(pos2, y_sorted, x1, route, mod3)


def _routing_tables(route, n):
    t = T_MOE
    flat_e = route[:, 2:4].astype(jnp.int32).reshape(-1)
    onehot = (flat_e[:, None] == jnp.arange(N_EXPERTS, dtype=jnp.int32)[None, :]).astype(jnp.int32)
    csum = jnp.cumsum(onehot, axis=0)
    rank = jnp.sum(onehot * csum, axis=1) - 1
    counts = csum[-1]
    ptiles = (counts + t - 1) // t
    tile_end = jnp.cumsum(ptiles)
    tile_start = tile_end - ptiles
    pos = jnp.sum(onehot * (tile_start * t)[None, :], axis=1) + rank
    n_tiles = (TOP_K * n) // t + N_EXPERTS
    n_valid = tile_end[-1]
    tile_ids = jnp.arange(n_tiles, dtype=jnp.int32)
    te = jnp.sum((tile_ids[:, None] >= tile_end[None, :]).astype(jnp.int32), axis=1)
    te_last = jnp.sum((n_valid - 1 >= tile_end).astype(jnp.int32))
    tile_expert = jnp.minimum(jnp.where(tile_ids < n_valid, te, te_last), N_EXPERTS - 1)
    pos8 = (pos * TOKEN_PITCH).astype(jnp.int32).reshape(n, TOP_K)
    blocks = lambda tb: pos8.reshape(n // tb, tb, TOP_K).transpose(0, 2, 1).reshape(n // tb, TOP_K * tb)
    pad_row8 = ((tile_start * t + counts) * TOKEN_PITCH).astype(jnp.int32)
    pad_cnt = (ptiles * t - counts).astype(jnp.int32)
    return (tile_expert.astype(jnp.int32), n_valid.astype(jnp.int32).reshape(1), pad_row8, pad_cnt,
            blocks(TB_DISPATCH), blocks(TM_COMB), n_tiles * t)


def _pad_lanes(a, left, total):
    pad = [(0, 0)] * (a.ndim - 1) + [(left, total - left - a.shape[-1])]
    return jnp.pad(a, pad)


def _layer(x2, c, cos_t, sin_t, b, s, w_ada, b_ada, norm_mix, w_in, conv_w, conv_b, conv_ln_g, conv_ln_b,
           q_a_norm, w_q_b, kv_a_norm, w_kv_b, q_norm, k_norm, w_out, norm_ffn, w_group, b_group,
           w_expert, b_expert, w_gate_e, w_up_e, w_down_e):
    n, d = x2.shape
    c_conv = conv_w.shape[1]
    q_rank = q_a_norm.shape[0]
    kv_rank = kv_a_norm.shape[0]
    row = lambda a: a.reshape(1, -1)

    o1 = 2 * c_conv
    o2 = o1 + q_rank
    o3 = o2 + kv_rank
    win_p = jnp.concatenate([w_in[:, :o3], _pad_lanes(w_in[:, o3:], QK_NOPE_DIM, LANES)], axis=1).astype(BF16)
    wq_p = _pad_lanes(w_q_b.reshape(q_rank, N_HEADS, QK_HEAD_DIM), 0, LANES).reshape(q_rank, N_HEADS * LANES)
    wkv = w_kv_b.reshape(kv_rank, N_HEADS, QK_NOPE_DIM + V_HEAD_DIM)
    wk_p = _pad_lanes(wkv[..., :QK_NOPE_DIM], 0, LANES).reshape(kv_rank, N_HEADS * LANES)
    wv = wkv[..., QK_NOPE_DIM:].reshape(kv_rank, N_HEADS * V_HEAD_DIM)
    gain_t = lambda g: jnp.broadcast_to(_pad_lanes(row(g), 0, LANES).reshape(LANES, 1), (LANES, TM_PROJ))
    qn_t = gain_t(q_norm)
    kn_t = gain_t(k_norm)
    wr = _pad_lanes(jnp.concatenate([w_expert, w_group], axis=1), 0, LANES).astype(BF16)
    br = _pad_lanes(row(jnp.concatenate([b_expert, b_group])), 0, LANES)
    conv_w_p = jnp.pad(conv_w, ((0, CONV_HALO - CONV_K), (0, 0)))

    mod3 = _ada(c, w_ada, b_ada).reshape(b, 6, d)
    u, qt4, k4, vt3 = _inproj(x2, mod3, row(norm_mix), win_p, row(q_a_norm), wq_p.T.astype(BF16),
                              row(kv_a_norm), wk_p.T.astype(BF16), wv.T.astype(BF16), qn_t, kn_t,
                              cos_t, sin_t, b, s)
    y_conv = _conv(u.reshape(b, s, c_conv), conv_w_p, row(conv_b), row(conv_ln_g), row(conv_ln_b))
    y_attn = _attn(qt4, k4, vt3)
    x1, route = _outproj(y_conv.reshape(n, c_conv), y_attn.reshape(n, -1), x2, mod3,
                         w_out.astype(BF16), row(norm_ffn), wr, br, s)
    tile_expert, n_valid, pad_row8, pad_cnt, pos8_disp, pos8_comb, n_rows = _routing_tables(route, n)
    xs = _dispatch(pad_row8, pad_cnt, n_valid, pos8_disp, x1, mod3, row(norm_ffn), n_rows, s)
    ys = _moe(tile_expert, n_valid, xs, w_gate_e, w_up_e, w_down_e)
    return _combine(pos8_comb, ys, x1, route, mod3, s)


def kernel(x, c, positions, w_ada, b_ada, norm_mix, w_in, conv_w, conv_b, conv_ln_g, conv_ln_b, q_a_norm,
           w_q_b, kv_a_norm, w_kv_b, q_norm, k_norm, w_out, norm_ffn, w_group, b_group, w_expert, b_expert,
           w_gate_e, w_up_e, w_down_e):
    b, s, d = x.shape
    n = b * s
    inv_freq = ROPE_THETA ** (-jnp.arange(0, QK_ROPE_DIM, 2, dtype=F32) / QK_ROPE_DIM)
    ang = inv_freq[:, None] * positions.astype(F32).reshape(1, n)
    cos_t, sin_t = jnp.cos(ang), jnp.sin(ang)
    x2 = x.reshape(n, d)
    for l in range(w_ada.shape[0]):
        x2 = _layer(x2, c, cos_t, sin_t, b, s, w_ada[l], b_ada[l], norm_mix[l], w_in[l], conv_w[l], conv_b[l],
                    conv_ln_g[l], conv_ln_b[l], q_a_norm[l], w_q_b[l], kv_a_norm[l], w_kv_b[l], q_norm[l],
                    k_norm[l], w_out[l], norm_ffn[l], w_group[l], b_group[l], w_expert[l], b_expert[l],
                    w_gate_e[l], w_up_e[l], w_down_e[l])
    return x2.reshape(b, s, d)
```
